```python
import functools
import jax
import jax.numpy as jnp
from jax import lax
import numpy as np

D_MODEL = 4096
BATCH = 4
SEQ = 2048
DEPTH = 4
DEC_BATCH = 128
DEC_SEQ = 4
PAST_LEN = 8192
PAGE_SIZE = 128

RET_HEADS = D_MODEL // 512
RET_DK = 128
RET_DV = 128
RET_W = RET_HEADS * RET_DV
RET_CHUNK = 128
POOL_WINDOWS = (2, 4, 8, 16)
POOL_GROUPS = 4
POOL_W = D_MODEL // 4
POOL_GC = POOL_W // POOL_GROUPS
POOL_BUF = 15
MLA_HEADS = D_MODEL // 512
MLA_NOPE = 128
MLA_ROPE = 64
MLA_V = 128
MLA_QK = MLA_NOPE + MLA_ROPE
MLA_SCALE = MLA_QK ** -0.5
Q_LORA = 3 * D_MODEL // 16
KV_LORA = D_MODEL // 16
MLA_ROW = KV_LORA + MLA_ROPE
Q_BLOCK = 128
N_BRANCH = 3
N_GROUPS = 4
EXPERTS_PER_GROUP = 4
N_EXPERTS = N_GROUPS * EXPERTS_PER_GROUP
TOP_K = 2
D_EXPERT = 3 * D_MODEL // 16
ROPE_BASE = 10000.0
EPS = 1e-6
NEG_INF = -1e30
IN_SIZES = (RET_HEADS * RET_DK, RET_HEADS * RET_DK, RET_W, RET_W, POOL_W, Q_LORA, KV_LORA, MLA_ROPE, N_BRANCH * D_MODEL)
N_IN = sum(IN_SIZES)

kernel_name = 'hybrid_retention_pool_mla_hmoe_step'

F32 = jnp.float32


def rmsnorm(x, g):
    xf = x.astype(F32)
    y = xf * lax.rsqrt(jnp.mean(xf * xf, axis=-1, keepdims=True) + EPS)
    return (y * g.astype(F32)).astype(x.dtype)


def rope(x, pos):
    half = x.shape[-1] // 2
    inv = ROPE_BASE ** (-jnp.arange(half, dtype=F32) / half)
    ang = pos.astype(F32)[:, None] * inv[None, :]
    cos = jnp.cos(ang)[:, None, :]
    sin = jnp.sin(ang)[:, None, :]
    x1, x2 = x[..., :half], x[..., half:]
    return jnp.concatenate([x1 * cos - x2 * sin, x2 * cos + x1 * sin], axis=-1).astype(x.dtype)


def in_projection(h, w_in):
    z = h @ w_in
    return jnp.split(z, np.cumsum(IN_SIZES)[:-1].tolist(), axis=-1)


def ret_log_decay():
    return jnp.log1p(-jnp.exp2(-5.0 - jnp.arange(RET_HEADS, dtype=F32)))


def retention_block(q, k, v, s_prev):
    lg = ret_log_decay()
    L = q.shape[1]
    idx = jnp.arange(L)
    diff = idx[:, None] - idx[None, :]
    decay = jnp.where(diff >= 0, jnp.exp(lg[:, None, None] * jnp.maximum(diff, 0)), 0.0)
    inner = jnp.einsum('blhd,bmhd->bhlm', q, k) * decay[None]
    o = jnp.einsum('bhlm,bmhe->blhe', inner, v)
    q_decay = jnp.exp(lg[:, None] * (idx + 1)[None, :])
    o = o + jnp.einsum('blhd,bhde->blhe', q, s_prev) * q_decay.T[None, :, :, None]
    k_decay = jnp.exp(lg[:, None] * (L - 1 - idx)[None, :])
    s_new = jnp.exp(lg * L)[None, :, None, None] * s_prev + jnp.einsum('blhd,blhe,hl->bhde', k, v, k_decay)
    return o, s_new


def retention_prompt(q, k, v):
    B, S = q.shape[:2]
    nc = S // RET_CHUNK

    def chunks(a):
        return a.reshape((B, nc, RET_CHUNK) + a.shape[2:]).swapaxes(0, 1)

    def step(s, blk):
        qc, kc, vc = blk
        o, s = retention_block(qc, kc, vc, s)
        return s, o

    s0 = jnp.zeros((B, RET_HEADS, RET_DK, RET_DV), F32)
    s_fin, o = lax.scan(step, s0, (chunks(q), chunks(k), chunks(v)))
    return o.swapaxes(0, 1).reshape(B, S, RET_HEADS, RET_DV), s_fin


def head_group_norm(o, g):
    of = o.astype(F32)
    mu = jnp.mean(of, axis=-1, keepdims=True)
    var = jnp.mean(jnp.square(of - mu), axis=-1, keepdims=True)
    y = (of - mu) * lax.rsqrt(var + EPS)
    return y.reshape(o.shape[0], o.shape[1], -1) * g.astype(F32)


def pooling_mixer(u, buf, pos0, pool_w, pool_scale):
    B, L, _ = u.shape
    full = jnp.concatenate([buf.astype(u.dtype), u], axis=1)
    cs = jnp.cumsum(full.astype(F32), axis=1)
    cs = jnp.concatenate([jnp.zeros((B, 1, POOL_W), F32), cs], axis=1)
    t = jnp.arange(L)
    end = POOL_BUF + 1 + t
    pos = pos0 + t
    means = []
    for gi, w in enumerate(POOL_WINDOWS):
        c = slice(gi * POOL_GC, (gi + 1) * POOL_GC)
        win = cs[:, end, c] - cs[:, end - w, c]
        cnt = jnp.minimum(w, pos + 1).astype(F32)
        means.append(win / cnt[None, :, None])
    pooled = (jnp.concatenate(means, axis=-1) - u.astype(F32)).astype(u.dtype)
    z = jnp.einsum('blgc,gcd->blgd', pooled.reshape(B, L, POOL_GROUPS, POOL_GC), pool_w)
    return z.reshape(B, L, POOL_W) * pool_scale, full[:, -POOL_BUF:]


def qk_gain(g):
    r = g[MLA_NOPE:]
    return jnp.concatenate([g[:MLA_NOPE], r, r])


def mla_queries(c_q, pos, qnorm_g, w_uq, qn_g):
    B, L, _ = c_q.shape
    q = (rmsnorm(c_q, qnorm_g) @ w_uq).reshape(B, L, MLA_HEADS, MLA_QK)
    q = jnp.concatenate([q[..., :MLA_NOPE], rope(q[..., MLA_NOPE:], pos)], axis=-1)
    return rmsnorm(q, qk_gain(qn_g))


def mla_keys(rows, w_uk, kn_g):
    lat, k_pe = rows[..., :KV_LORA], rows[..., KV_LORA:]
    k_nope = jnp.einsum('bsc,chd->bshd', lat, w_uk)
    k_pe = jnp.broadcast_to(k_pe[:, :, None, :], k_nope.shape[:3] + (MLA_ROPE,)).astype(k_nope.dtype)
    return rmsnorm(jnp.concatenate([k_nope, k_pe], axis=-1), qk_gain(kn_g))


def mla_attend_prompt(q, rows, w_uk, w_uv, kn_g):
    B, S = q.shape[:2]
    k = mla_keys(rows, w_uk, kn_g)
    v = jnp.einsum('bsc,chd->bshd', rows[..., :KV_LORA], w_uv)
    nb = S // Q_BLOCK
    qb = q.reshape(B, nb, Q_BLOCK, MLA_HEADS, MLA_QK).swapaxes(0, 1)
    kpos = jnp.arange(S)

    def block(args):
        qi, bi = args
        qpos = bi * Q_BLOCK + jnp.arange(Q_BLOCK)
        s = jnp.einsum('bqhd,bkhd->bhqk', qi, k).astype(F32) * MLA_SCALE
        s = jnp.where(kpos[None, :] <= qpos[:, None], s, NEG_INF)
        p = jax.nn.softmax(s, axis=-1).astype(v.dtype)
        return jnp.einsum('bhqk,bkhd->bqhd', p, v)

    o = lax.map(block, (qb, jnp.arange(nb)))
    return o.swapaxes(0, 1).reshape(B, S, MLA_HEADS, MLA_V)


def mla_attend_sample(q, rows, w_uk, w_uv, kn_g, cache, layer, page_table):
    Bd, T = q.shape[:2]

    def scores(k):
        return jnp.einsum('bqhd,bkhd->bhqk', q, k).astype(F32) * MLA_SCALE

    def absorb(carry, s, lat):
        m, l, acc = carry
        m_new = jnp.maximum(m, jnp.max(s, axis=-1))
        corr = jnp.exp(m - m_new)
        p = jnp.exp(s - m_new[..., None])
        acc = acc * corr[..., None] + jnp.einsum('bhqk,bkc->bhqc', p, lat.astype(F32))
        return (m_new, l * corr + jnp.sum(p, axis=-1), acc)

    def page_step(carry, phys):
        past = cache[layer, phys]
        return absorb(carry, scores(mla_keys(past, w_uk, kn_g)), past[..., :KV_LORA]), None

    init = (jnp.full((Bd, MLA_HEADS, T), NEG_INF, F32),
            jnp.zeros((Bd, MLA_HEADS, T), F32),
            jnp.zeros((Bd, MLA_HEADS, T, KV_LORA), F32))
    carry, _ = lax.scan(page_step, init, page_table.T)
    causal = jnp.tril(jnp.ones((T, T), dtype=bool))
    s_self = jnp.where(causal, scores(mla_keys(rows, w_uk, kn_g)), NEG_INF)
    m, l, acc = absorb(carry, s_self, rows[..., :KV_LORA])
    o_lat = acc / l[..., None]
    return jnp.einsum('bhqc,chd->bqhd', o_lat, w_uv.astype(F32)).astype(q.dtype)


def token_mixers(h, pos0, lw, ret_state, pool_buf, attend):
    B, L, _ = h.shape
    pos = pos0 + jnp.arange(L)
    q_r, k_r, v_r, g_r, u, c_q, c_kv, k_pe, gate_logits = in_projection(h, lw['w_in'])
    q = rope(q_r.reshape(B, L, RET_HEADS, RET_DK), pos)
    k = rope(k_r.reshape(B, L, RET_HEADS, RET_DK), pos) * RET_DK ** -0.5
    v = v_r.reshape(B, L, RET_HEADS, RET_DV)
    if ret_state is None:
        o_r, s_new = retention_prompt(q, k, v)
    else:
        o_r, s_new = retention_block(q, k, v, ret_state.astype(F32))
    o_r = head_group_norm(o_r, lw['ret_gn_g']).astype(h.dtype) * jax.nn.silu(g_r)
    o_p, buf_new = pooling_mixer(u, pool_buf, pos0, lw['pool_w'], lw['pool_scale'])
    q_m = mla_queries(c_q, pos, lw['mla_qnorm_g'], lw['mla_w_uq'], lw['mla_qn_g'])
    k_pe_rot = rope(k_pe[:, :, None, :], pos)[:, :, 0, :]
    rows = jnp.concatenate([rmsnorm(c_kv, lw['mla_kvnorm_g']), k_pe_rot], axis=-1)
    w_ukv = lw['mla_w_ukv'].reshape(KV_LORA, MLA_HEADS, MLA_NOPE + MLA_V)
    o_m = attend(q_m, rows, w_ukv[..., :MLA_NOPE], w_ukv[..., MLA_NOPE:], lw['mla_kn_g'])
    o_m = o_m.reshape(B, L, MLA_HEADS * MLA_V)
    gates = jax.nn.sigmoid(gate_logits.astype(F32)).astype(h.dtype).reshape(B, L, N_BRANCH, D_MODEL)
    merged = (gates[:, :, 0] * (o_r @ lw['w_br_ret'])
              + gates[:, :, 1] * (o_p @ lw['w_br_pool'])
              + gates[:, :, 2] * (o_m @ lw['w_br_mla']))
    return merged @ lw['w_out'], rows, s_new.astype(h.dtype), buf_new


def hier_moe(h, wg, bg, we, be, w_gate, w_up, w_down):
    B, L, D = h.shape
    t = h.reshape(B * L, D)
    g_logits = (t @ wg).astype(F32) + bg.astype(F32)
    g_prob = jax.nn.softmax(g_logits, axis=-1)
    _, g_idx = lax.top_k(g_logits, 1)
    g_w = jnp.take_along_axis(g_prob, g_idx, axis=1)
    e_logits = ((t @ we).astype(F32) + be.astype(F32)).reshape(-1, N_GROUPS, EXPERTS_PER_GROUP)
    e_in = jnp.take_along_axis(e_logits, g_idx[:, :, None], axis=1)[:, 0]
    e_top, e_idx = lax.top_k(jax.nn.softmax(e_in, axis=-1), TOP_K)
    w = g_w * e_top / jnp.sum(e_top, axis=-1, keepdims=True)
    ids = g_idx * EXPERTS_PER_GROUP + e_idx
    combine = jnp.einsum('nk,nke->ne', w, jax.nn.one_hot(ids, N_EXPERTS, dtype=F32))
    out = jnp.zeros((t.shape[0], D), F32)
    for e in range(N_EXPERTS):
        a = jax.nn.silu(t @ w_gate[e]) * (t @ w_up[e])
        out = out + combine[:, e:e + 1] * (a @ w_down[e]).astype(F32)
    return out.reshape(B, L, D).astype(h.dtype)


def setup_inputs(seed: int = 0) -> dict:
    key = jax.random.key(seed)
    ks = iter(jax.random.split(key, 40))

    def nrm(shape, scale):
        return jax.random.normal(next(ks), shape, F32) * scale

    def gain(shape):
        return 1.0 + nrm(shape, 0.02)

    n_pages = PAST_LEN // PAGE_SIZE
    n_pool = (DEC_BATCH * n_pages * 5) // 4
    page_table = jax.random.permutation(next(ks), n_pool)[:DEC_BATCH * n_pages].reshape(DEC_BATCH, n_pages).astype(jnp.int32)
    res = (2 * DEPTH) ** -0.5
    return {
        'x_prompt': nrm((BATCH, SEQ, D_MODEL), 1.0),
        'x_sample': nrm((DEC_BATCH, DEC_SEQ, D_MODEL), 1.0),
        'cache_mla': nrm((DEPTH, n_pool, PAGE_SIZE, MLA_ROW), 1.0),
        'page_table': page_table,
        'state_ret': nrm((DEPTH, DEC_BATCH, RET_HEADS, RET_DK, RET_DV), 1.0),
        'state_pool': nrm((DEPTH, DEC_BATCH, POOL_BUF, POOL_W), 1.0),
        'norm1_g': gain((DEPTH, D_MODEL)),
        'w_in': nrm((DEPTH, D_MODEL, N_IN), D_MODEL ** -0.5),
        'ret_gn_g': gain((DEPTH, RET_W)),
        'pool_w': nrm((DEPTH, POOL_GROUPS, POOL_GC, POOL_GC), POOL_GC ** -0.5),
        'pool_scale': 1.0 + nrm((DEPTH, POOL_W), 0.1),
        'mla_qnorm_g': gain((DEPTH, Q_LORA)),
        'mla_w_uq': nrm((DEPTH, Q_LORA, MLA_HEADS * MLA_QK), Q_LORA ** -0.5),
        'mla_kvnorm_g': gain((DEPTH, KV_LORA)),
        'mla_w_ukv': nrm((DEPTH, KV_LORA, MLA_HEADS * (MLA_NOPE + MLA_V)), KV_LORA ** -0.5),
        'mla_qn_g': gain((DEPTH, MLA_NOPE + MLA_ROPE // 2)),
        'mla_kn_g': gain((DEPTH, MLA_NOPE + MLA_ROPE // 2)),
        'w_br_ret': nrm((DEPTH, RET_W, D_MODEL), RET_W ** -0.5),
        'w_br_pool': nrm((DEPTH, POOL_W, D_MODEL), POOL_W ** -0.5),
        'w_br_mla': nrm((DEPTH, MLA_HEADS * MLA_V, D_MODEL), (MLA_HEADS * MLA_V) ** -0.5),
        'w_out': nrm((DEPTH, D_MODEL, D_MODEL), D_MODEL ** -0.5 * res),
        'norm2_g': gain((DEPTH, D_MODEL)),
        'router_g_w': nrm((DEPTH, D_MODEL, N_GROUPS), D_MODEL ** -0.5),
        'router_g_b': nrm((DEPTH, N_GROUPS), 0.01),
        'router_e_w': nrm((DEPTH, D_MODEL, N_EXPERTS), D_MODEL ** -0.5),
        'router_e_b': nrm((DEPTH, N_EXPERTS), 0.01),
        'exp_w_gate': nrm((DEPTH, N_EXPERTS, D_MODEL, D_EXPERT), D_MODEL ** -0.5),
        'exp_w_up': nrm((DEPTH, N_EXPERTS, D_MODEL, D_EXPERT), D_MODEL ** -0.5),
        'exp_w_down': nrm((DEPTH, N_EXPERTS, D_EXPERT, D_MODEL), D_EXPERT ** -0.5 * res),
    }


def reference(x_prompt, x_sample, cache_mla, page_table, state_ret, state_pool,
              norm1_g, w_in, ret_gn_g, pool_w, pool_scale,
              mla_qnorm_g, mla_w_uq, mla_kvnorm_g, mla_w_ukv, mla_qn_g, mla_kn_g,
              w_br_ret, w_br_pool, w_br_mla, w_out, norm2_g,
              router_g_w, router_g_b, router_e_w, router_e_b,
              exp_w_gate, exp_w_up, exp_w_down):
    xp, xs = x_prompt, x_sample
    rows_p, rows_s, ret_p, ret_s, pool_p, pool_s = [], [], [], [], [], []
    for l in range(DEPTH):
        lw = {'w_in': w_in[l], 'ret_gn_g': ret_gn_g[l], 'pool_w': pool_w[l], 'pool_scale': pool_scale[l],
              'mla_qnorm_g': mla_qnorm_g[l], 'mla_w_uq': mla_w_uq[l], 'mla_kvnorm_g': mla_kvnorm_g[l],
              'mla_w_ukv': mla_w_ukv[l], 'mla_qn_g': mla_qn_g[l], 'mla_kn_g': mla_kn_g[l],
              'w_br_ret': w_br_ret[l], 'w_br_pool': w_br_pool[l], 'w_br_mla': w_br_mla[l], 'w_out': w_out[l]}
        moe_w = (router_g_w[l], router_g_b[l], router_e_w[l], router_e_b[l], exp_w_gate[l], exp_w_up[l], exp_w_down[l])
        sample_attend = functools.partial(mla_attend_sample, cache=cache_mla, layer=l, page_table=page_table)
        zero_buf = jnp.zeros((xp.shape[0], POOL_BUF, POOL_W), xp.dtype)
        y, r, s, b = token_mixers(rmsnorm(xp, norm1_g[l]), 0, lw, None, zero_buf, mla_attend_prompt)
        xp = xp + y
        xp = xp + hier_moe(rmsnorm(xp, norm2_g[l]), *moe_w)
        rows_p.append(r)
        ret_p.append(s)
        pool_p.append(b)
        y, r, s, b = token_mixers(rmsnorm(xs, norm1_g[l]), PAST_LEN, lw, state_ret[l], state_pool[l], sample_attend)
        xs = xs + y
        xs = xs + hier_moe(rmsnorm(xs, norm2_g[l]), *moe_w)
        rows_s.append(r)
        ret_s.append(s)
        pool_s.append(b)
    return (xp, xs, jnp.stack(rows_p), jnp.stack(rows_s), jnp.stack(ret_p), jnp.stack(ret_s), jnp.stack(pool_p), jnp.stack(pool_s))
```

```python
import functools

import jax
import jax.numpy as jnp
from jax import lax
from jax.experimental import pallas as pl
from jax.experimental.pallas import tpu as pltpu

F32 = jnp.float32
BF16 = jnp.bfloat16
I32 = jnp.int32

EPS = 1e-6
ROPE_BASE = 10000.0
NEG_INF = -1e30
POOL_WINDOWS = (2, 4, 8, 16)
N_GROUPS = 4
EXPERTS_PER_GROUP = 4
LANES = 128
VMEM_LIMIT = 56 * 1024 * 1024
ROUTE_LANES = 128
MOE_TM = 256


def _cp(*sem):
    return pltpu.CompilerParams(dimension_semantics=sem, vmem_limit_bytes=VMEM_LIMIT)


def _tile(n, pref, mult=8):
    for t in range(min(pref, n), 0, -1):
        if n % t == 0 and t % mult == 0:
            return t
    return n


def _dot(a, b):
    return jnp.dot(a, b, preferred_element_type=F32)


def _dot_nt(a, b):
    return lax.dot_general(a, b, (((1,), (1,)), ((), ())), preferred_element_type=F32)


def _idiv(x, d):
    return jnp.floor((x.astype(F32) + 0.5) * (1.0 / d)).astype(I32)


def _rmsnorm_kernel(x_ref, g_ref, o_ref):
    x = x_ref[...]
    y = x * lax.rsqrt(jnp.mean(x * x, axis=-1, keepdims=True) + EPS)
    o_ref[...] = (y * g_ref[...]).astype(o_ref.dtype)


def _rmsnorm(x, g, out_dtype):
    T, D = x.shape
    tm = _tile(T, 512)
    return pl.pallas_call(
        _rmsnorm_kernel,
        grid=(T // tm,),
        in_specs=[pl.BlockSpec((tm, D), lambda i: (i, 0)), pl.BlockSpec((1, D), lambda i: (0, 0))],
        out_specs=pl.BlockSpec((tm, D), lambda i: (i, 0)),
        out_shape=jax.ShapeDtypeStruct((T, D), out_dtype),
        compiler_params=_cp("parallel"),
        name="rmsnorm",
    )(x, g.reshape(1, D))


def _mm_kernel(a_ref, b_ref, *rest, has_res):
    if has_res:
        r_ref, o_ref = rest
    else:
        (o_ref,) = rest
    acc = _dot(a_ref[...], b_ref[...])
    if has_res:
        acc = acc + r_ref[...]
    o_ref[...] = acc.astype(o_ref.dtype)


def _matmul(a, b, *, tm, tn, res=None, out_dtype=F32, name="matmul"):
    M, K = a.shape
    N = b.shape[1]
    in_specs = [pl.BlockSpec((tm, K), lambda j, i: (i, 0)), pl.BlockSpec((K, tn), lambda j, i: (0, j))]
    args = [a, b]
    if res is not None:
        in_specs.append(pl.BlockSpec((tm, tn), lambda j, i: (i, j)))
        args.append(res)
    return pl.pallas_call(
        functools.partial(_mm_kernel, has_res=res is not None),
        grid=(N // tn, M // tm),
        in_specs=in_specs,
        out_specs=pl.BlockSpec((tm, tn), lambda j, i: (i, j)),
        out_shape=jax.ShapeDtypeStruct((M, N), out_dtype),
        compiler_params=_cp("parallel", "parallel"),
        name=name,
    )(*args)


def _ret_core(q, k, v, s_prev, decay, qdec):
    qb = q.astype(BF16)
    inner = _dot_nt(qb, k.astype(BF16)) * decay
    o = _dot(inner.astype(BF16), v.astype(BF16))
    return o, qb


def _ret_epilogue(o, gn, gate):
    mu = jnp.mean(o, axis=-1, keepdims=True)
    d = o - mu
    var = jnp.mean(d * d, axis=-1, keepdims=True)
    y = d * lax.rsqrt(var + EPS) * gn
    return y * (gate * jax.nn.sigmoid(gate))


def _ret_prompt_kernel(lg_ref, q_ref, k_ref, v_ref, gr_ref, cos_ref, sin_ref, gn_ref,
                       o_ref, sfin_ref, st_ref, *, C, DK):
    h = pl.program_id(1)
    c = pl.program_id(2)
    lg = lg_ref[h]

    @pl.when(c == 0)
    def _():
        st_ref[...] = jnp.zeros_like(st_ref)

    cos = cos_ref[...]
    sin = sin_ref[...]
    q = q_ref[...]
    k = k_ref[...]
    q = q * cos + pltpu.roll(q, DK // 2, 1) * sin
    k = (k * cos + pltpu.roll(k, DK // 2, 1) * sin) * (DK ** -0.5)
    v = v_ref[...]
    r = lax.broadcasted_iota(I32, (C, 1), 0)
    col = lax.broadcasted_iota(I32, (1, C), 1)
    diff = r - col
    decay = jnp.where(diff >= 0, jnp.exp(lg * jnp.maximum(diff, 0).astype(F32)), 0.0)
    s_prev = st_ref[...]
    o, qb = _ret_core(q, k, v, s_prev, decay, None)
    o = o + _dot(qb, s_prev.astype(BF16)) * jnp.exp(lg * (r + 1).astype(F32))
    kd = k * jnp.exp(lg * (C - 1 - r).astype(F32))
    s_new = jnp.exp(lg * C) * s_prev + _dot(kd.T.astype(BF16), v.astype(BF16))
    st_ref[...] = s_new
    sfin_ref[0, 0] = s_new
    o_ref[...] = _ret_epilogue(o, gn_ref[...], gr_ref[...]).astype(o_ref.dtype)


def _ret_sample_kernel(lg_ref, q_ref, k_ref, v_ref, gr_ref, cos_ref, sin_ref, gn_ref, s_ref,
                       o_ref, snew_ref, *, NB, LD, DK):
    h = pl.program_id(1)
    lg = lg_ref[h]
    L = NB * LD
    cos = cos_ref[...]
    sin = sin_ref[...]
    q = q_ref[...]
    k = k_ref[...]
    q = q * cos + pltpu.roll(q, DK // 2, 1) * sin
    k = (k * cos + pltpu.roll(k, DK // 2, 1) * sin) * (DK ** -0.5)
    v = v_ref[...]
    r = lax.broadcasted_iota(I32, (L, 1), 0)
    col = lax.broadcasted_iota(I32, (1, L), 1)
    rb = _idiv(r, LD)
    rt = r - rb * LD
    same = rb == _idiv(col, LD)
    diff = r - col
    decay = jnp.where(same & (diff >= 0), jnp.exp(lg * jnp.maximum(diff, 0).astype(F32)), 0.0)
    o, qb = _ret_core(q, k, v, None, decay, None)
    qdec = jnp.exp(lg * (rt + 1).astype(F32))
    kd = k * jnp.exp(lg * (LD - 1 - rt).astype(F32))
    vb = v.astype(BF16)
    sdec = jnp.exp(lg * LD)
    for s in range(NB):
        mine = rb == s
        s_prev = s_ref[s, 0]
        o = o + jnp.where(mine, _dot(qb, s_prev.astype(BF16)) * qdec, 0.0)
        kds = jnp.where(mine, kd, 0.0)
        snew_ref[s, 0] = sdec * s_prev + _dot(kds.T.astype(BF16), vb)
    o_ref[...] = _ret_epilogue(o, gn_ref[...], gr_ref[...]).astype(o_ref.dtype)


def _retention(zA, lg, gn, tabs_p, tabs_s, state, dm):
    H, DK, DV, B, S, Bd, LD, Tp = dm["H"], dm["DK"], dm["DV"], dm["B"], dm["S"], dm["Bd"], dm["LD"], dm["Tp"]
    C = _tile(S, 256)
    nc = S // C
    gn2 = gn.reshape(1, H * DV)
    o_p, s_p = pl.pallas_call(
        functools.partial(_ret_prompt_kernel, C=C, DK=DK),
        grid_spec=pltpu.PrefetchScalarGridSpec(
            num_scalar_prefetch=1,
            grid=(B, H, nc),
            in_specs=[
                pl.BlockSpec((C, DK), lambda b, h, c, lg: (b * nc + c, h)),
                pl.BlockSpec((C, DK), lambda b, h, c, lg: (b * nc + c, H + h)),
                pl.BlockSpec((C, DV), lambda b, h, c, lg: (b * nc + c, 2 * H + h)),
                pl.BlockSpec((C, DV), lambda b, h, c, lg: (b * nc + c, 3 * H + h)),
                pl.BlockSpec((C, DK), lambda b, h, c, lg: (c, 0)),
                pl.BlockSpec((C, DK), lambda b, h, c, lg: (c, 0)),
                pl.BlockSpec((1, DV), lambda b, h, c, lg: (0, h)),
            ],
            out_specs=[
                pl.BlockSpec((C, DV), lambda b, h, c, lg: (b * nc + c, h)),
                pl.BlockSpec((1, 1, DK, DV), lambda b, h, c, lg: (b, h, 0, 0)),
            ],
            scratch_shapes=[pltpu.VMEM((DK, DV), F32)],
        ),
        out_shape=[jax.ShapeDtypeStruct((Tp, H * DV), BF16),
                   jax.ShapeDtypeStruct((B, H, DK, DV), F32)],
        compiler_params=_cp("parallel", "parallel", "arbitrary"),
        name="retention_prompt",
    )(lg, zA, zA, zA, zA, tabs_p[0], tabs_p[1], gn2)

    NB = _tile(Bd, 8, 1)
    L = NB * LD
    off = Tp // L
    o_s, s_s = pl.pallas_call(
        functools.partial(_ret_sample_kernel, NB=NB, LD=LD, DK=DK),
        grid_spec=pltpu.PrefetchScalarGridSpec(
            num_scalar_prefetch=1,
            grid=(Bd // NB, H),
            in_specs=[
                pl.BlockSpec((L, DK), lambda i, h, lg: (off + i, h)),
                pl.BlockSpec((L, DK), lambda i, h, lg: (off + i, H + h)),
                pl.BlockSpec((L, DV), lambda i, h, lg: (off + i, 2 * H + h)),
                pl.BlockSpec((L, DV), lambda i, h, lg: (off + i, 3 * H + h)),
                pl.BlockSpec((L, DK), lambda i, h, lg: (0, 0)),
                pl.BlockSpec((L, DK), lambda i, h, lg: (0, 0)),
                pl.BlockSpec((1, DV), lambda i, h, lg: (0, h)),
                pl.BlockSpec((NB, 1, DK, DV), lambda i, h, lg: (i, h, 0, 0)),
            ],
            out_specs=[
                pl.BlockSpec((L, DV), lambda i, h, lg: (i, h)),
                pl.BlockSpec((NB, 1, DK, DV), lambda i, h, lg: (i, h, 0, 0)),
            ],
        ),
        out_shape=[jax.ShapeDtypeStruct((Bd * LD, H * DV), BF16),
                   jax.ShapeDtypeStruct((Bd, H, DK, DV), F32)],
        compiler_params=_cp("parallel", "parallel"),
        name="retention_sample",
    )(lg, zA, zA, zA, zA, tabs_s[0], tabs_s[1], gn2, state)
    return jnp.concatenate([o_p, o_s], axis=0), s_p, s_s


def _pool_prompt_kernel(u_ref, halo_ref, w_ref, sc_ref, o_ref, full_ref, *, TL, HALO, GC):
    i = pl.program_id(1)
    u = u_ref[...]
    full_ref[pl.ds(HALO, TL), :] = u
    full_ref[pl.ds(0, HALO), :] = jnp.where(i == 0, 0.0, halo_ref[...])
    pos = i * TL + lax.broadcasted_iota(I32, (TL, 1), 0)
    for g, w in enumerate(POOL_WINDOWS):
        cs = slice(g * GC, (g + 1) * GC)
        win = full_ref[pl.ds(HALO, TL), cs]
        for j in range(1, w):
            win = win + full_ref[pl.ds(HALO - j, TL), cs]
        cnt = jnp.minimum(w, pos + 1).astype(F32)
        pooled = win / cnt - u[:, cs]
        z = _dot(pooled.astype(BF16), w_ref[g])
        o_ref[:, cs] = (z * sc_ref[:, cs]).astype(o_ref.dtype)


def _pool_sample_kernel(u_ref, buf_ref, w_ref, sc_ref, o_ref, nbuf_ref, *, LD, PB, GC, pos0):
    rows = [buf_ref[:, j, :] for j in range(PB)] + [u_ref[:, t, :] for t in range(LD)]
    for j in range(PB):
        nbuf_ref[:, j, :] = rows[LD + j]
    for g, w in enumerate(POOL_WINDOWS):
        cs = slice(g * GC, (g + 1) * GC)
        pooled = []
        for t in range(LD):
            win = rows[PB + t][:, cs]
            for j in range(1, w):
                win = win + rows[PB + t - j][:, cs]
            cnt = float(min(w, pos0 + t + 1))
            pooled.append(win / cnt - rows[PB + t][:, cs])
        z = _dot(jnp.concatenate(pooled, axis=0).astype(BF16), w_ref[g])
        nb = u_ref.shape[0]
        for t in range(LD):
            o_ref[:, t, cs] = (z[t * nb:(t + 1) * nb] * sc_ref[:, cs]).astype(o_ref.dtype)


def _pooling(zA, pool_w, pool_scale, buf, dm):
    B, S, Bd, LD, Tp, PW, PB = dm["B"], dm["S"], dm["Bd"], dm["LD"], dm["Tp"], dm["PW"], dm["PB"]
    G = len(POOL_WINDOWS)
    GC = PW // G
    HALO = 16
    assert PB < HALO and S % HALO == 0 and max(POOL_WINDOWS) <= HALO
    TL = _tile(S, 512, HALO)
    nt = S // TL
    ucol = dm["u_off"] // PW
    wb = pool_w.astype(BF16)
    sc = pool_scale.reshape(1, PW)
    o_p = pl.pallas_call(
        functools.partial(_pool_prompt_kernel, TL=TL, HALO=HALO, GC=GC),
        grid=(B, nt),
        in_specs=[
            pl.BlockSpec((TL, PW), lambda b, i: (b * nt + i, ucol)),
            pl.BlockSpec((HALO, PW), lambda b, i: (jnp.maximum((b * nt + i) * (TL // HALO) - 1, 0), ucol)),
            pl.BlockSpec((G, GC, GC), lambda b, i: (0, 0, 0)),
            pl.BlockSpec((1, PW), lambda b, i: (0, 0)),
        ],
        out_specs=pl.BlockSpec((TL, PW), lambda b, i: (b * nt + i, 0)),
        out_shape=jax.ShapeDtypeStruct((Tp, PW), BF16),
        scratch_shapes=[pltpu.VMEM((TL + HALO, PW), F32)],
        compiler_params=_cp("parallel", "parallel"),
        name="pool_prompt",
    )(zA, zA, wb, sc)

    u_s = zA[Tp:, dm["u_off"]:dm["u_off"] + PW].reshape(Bd, LD, PW)
    NB = _tile(Bd, 32, 8)
    o_s, nbuf = pl.pallas_call(
        functools.partial(_pool_sample_kernel, LD=LD, PB=PB, GC=GC, pos0=dm["P0"]),
        grid=(Bd // NB,),
        in_specs=[
            pl.BlockSpec((NB, LD, PW), lambda i: (i, 0, 0)),
            pl.BlockSpec((NB, PB, PW), lambda i: (i, 0, 0)),
            pl.BlockSpec((G, GC, GC), lambda i: (0, 0, 0)),
            pl.BlockSpec((1, PW), lambda i: (0, 0)),
        ],
        out_specs=[pl.BlockSpec((NB, LD, PW), lambda i: (i, 0, 0)),
                   pl.BlockSpec((NB, PB, PW), lambda i: (i, 0, 0))],
        out_shape=[jax.ShapeDtypeStruct((Bd, LD, PW), BF16), jax.ShapeDtypeStruct((Bd, PB, PW), F32)],
        compiler_params=_cp("parallel"),
        name="pool_sample",
    )(u_s, buf, wb, sc)
    pool_p = zA[:Tp, dm["u_off"]:dm["u_off"] + PW].reshape(B, S, PW)[:, S - PB:]
    return jnp.concatenate([o_p, o_s.reshape(Bd * LD, PW)], axis=0), pool_p, nbuf


def _qprep_kernel(cq_ref, g_ref, w_ref, cos_ref, sa_ref, sb_ref, gn_ref, gr_ref, q_ref,
                  *, MH, NOPE, ROPE, scale):
    x = cq_ref[...]
    y = x * lax.rsqrt(jnp.mean(x * x, axis=-1, keepdims=True) + EPS) * g_ref[...]
    qq = _dot(y.astype(BF16), w_ref[...])
    qn = qq[:, :MH * NOPE]
    qr = qq[:, MH * NOPE:]
    W = MH * ROPE
    qr = (qr * cos_ref[...] + pltpu.roll(qr, W - ROPE // 2, 1) * sa_ref[...]
          + pltpu.roll(qr, ROPE // 2, 1) * sb_ref[...])
    gn = gn_ref[...]
    gr = gr_ref[...]
    for h in range(MH):
        a = qn[:, h * NOPE:(h + 1) * NOPE]
        b = qr[:, h * ROPE:(h + 1) * ROPE]
        ss = jnp.sum(a * a, axis=-1, keepdims=True) + jnp.sum(b * b, axis=-1, keepdims=True)
        inv = lax.rsqrt(ss * (1.0 / (NOPE + ROPE)) + EPS) * scale
        q_ref[h, :, :NOPE] = (a * inv * gn).astype(q_ref.dtype)
        q_ref[h, :, NOPE:] = (b * inv * gr).astype(q_ref.dtype)


def _kvprep_kernel(ckv_ref, kpe_ref, g_ref, wuk_ref, wuv_ref, cos_ref, sa_ref, sb_ref,
                   rows_ref, k_ref, v_ref, *, MH, NOPE, ROPE, KV):
    x = ckv_ref[...]
    lat = x * lax.rsqrt(jnp.mean(x * x, axis=-1, keepdims=True) + EPS) * g_ref[...]
    kp = kpe_ref[...]
    W = kp.shape[1]
    kp = (kp * cos_ref[...] + pltpu.roll(kp, W - ROPE // 2, 1) * sa_ref[...]
          + pltpu.roll(kp, ROPE // 2, 1) * sb_ref[...])
    rows_ref[:, :KV] = lat
    rows_ref[:, KV:] = kp[:, :ROPE]
    latb = lat.astype(BF16)
    kn = _dot(latb, wuk_ref[...])
    v_ref[...] = _dot(latb, wuv_ref[...]).astype(v_ref.dtype)
    sspe = jnp.sum(kp * kp, axis=-1, keepdims=True)
    for h in range(MH):
        a = kn[:, h * NOPE:(h + 1) * NOPE]
        ss = jnp.sum(a * a, axis=-1, keepdims=True) + sspe
        inv = lax.rsqrt(ss * (1.0 / (NOPE + ROPE)) + EPS)
        k_ref[h, :, :NOPE] = (a * inv).astype(k_ref.dtype)
        k_ref[h, :, NOPE:] = (kp[:, :ROPE] * inv).astype(k_ref.dtype)


def _attn_prompt_kernel(q_ref, k_ref, v_ref, o_ref, *, TQ, S):
    i = pl.program_id(2)
    s = _dot_nt(q_ref[0], k_ref[0])
    qpos = i * TQ + lax.broadcasted_iota(I32, (TQ, 1), 0)
    kpos = lax.broadcasted_iota(I32, (1, S), 1)
    s = jnp.where(kpos <= qpos, s, NEG_INF)
    m = jnp.max(s, axis=-1, keepdims=True)
    p = jnp.exp(s - m)
    p = p / jnp.sum(p, axis=-1, keepdims=True)
    o_ref[...] = _dot(p.astype(BF16), v_ref[...]).astype(o_ref.dtype)


def _attn_sample_kernel(pt_ref, qn_ref, qr_ref, self_ref, *rest, PP, MH, NOPE, ROPE, KV, LD, NP_STEPS):
    pages = rest[:PP]
    wuk_ref, wuv_ref, ind_ref, o_ref, m_ref, l_ref, acc_ref = rest[PP:]
    step = pl.program_id(1)
    NQ = LD * MH
    qn = qn_ref[0]
    qr = qr_ref[0]
    ind = ind_ref[...]
    ones_pe = jnp.ones((NQ, ROPE), BF16)

    def absorb(rows, mask):
        lat = rows[:, :KV]
        kpe = rows[:, KV:]
        latb = lat.astype(BF16)
        kn = _dot(latb, wuk_ref[...])
        ss = _dot_nt(ind, (kn * kn).astype(BF16)) + _dot_nt(ones_pe, (kpe * kpe).astype(BF16))
        rinv = lax.rsqrt(ss * (1.0 / (NOPE + ROPE)) + EPS)
        s = (_dot_nt(qn, kn.astype(BF16)) + _dot_nt(qr, kpe.astype(BF16))) * rinv
        if mask is not None:
            s = jnp.where(mask, s, NEG_INF)
        m_old = m_ref[...]
        m_new = jnp.maximum(m_old, jnp.max(s, axis=-1, keepdims=True))
        corr = jnp.exp(m_old - m_new)
        p = jnp.exp(s - m_new)
        l_ref[...] = l_ref[...] * corr + jnp.sum(p, axis=-1, keepdims=True)
        acc_ref[...] = acc_ref[...] * corr + _dot(p.astype(BF16), latb)
        m_ref[...] = m_new

    @pl.when(step == 0)
    def _():
        m_ref[...] = jnp.full_like(m_ref, NEG_INF)
        l_ref[...] = jnp.zeros_like(l_ref)
        acc_ref[...] = jnp.zeros_like(acc_ref)
        rows = self_ref[0]
        t = _idiv(lax.broadcasted_iota(I32, (NQ, 1), 0), MH)
        j = lax.broadcasted_iota(I32, (1, rows.shape[0]), 1)
        absorb(rows, j <= t)

    if PP == 1:
        absorb(pages[0][0, 0], None)
    else:
        absorb(jnp.concatenate([p[0, 0] for p in pages], axis=0), None)

    @pl.when(step == NP_STEPS - 1)
    def _():
        o_lat = (acc_ref[...] / l_ref[...]).astype(BF16)
        VD = wuv_ref.shape[1] // MH
        hrow = lax.broadcasted_iota(I32, (MH, MH * VD), 0)
        hcol = _idiv(lax.broadcasted_iota(I32, (MH, MH * VD), 1), VD)
        for t in range(LD):
            full = _dot(o_lat[t * MH:(t + 1) * MH], wuv_ref[...])
            o_ref[0, pl.ds(t, 1), :] = jnp.sum(jnp.where(hrow == hcol, full, 0.0), axis=0,
                                              keepdims=True).astype(o_ref.dtype)


def _mla(zA, lw, tabs_q, tabs_k, cache, layer, page_table, dm):
    T, Tp, B, S, Bd, LD = dm["T"], dm["Tp"], dm["B"], dm["S"], dm["Bd"], dm["LD"]
    MH, NOPE, ROPE, KV, QL, VD = dm["MH"], dm["NOPE"], dm["ROPE"], dm["KV"], dm["QL"], dm["VD"]
    QK = NOPE + ROPE
    ROW = KV + ROPE
    scale = QK ** -0.5
    tm = _tile(T, 512)

    w_uq = lw["mla_w_uq"].reshape(QL, MH, QK)
    w_uq = jnp.concatenate([w_uq[:, :, :NOPE].reshape(QL, MH * NOPE),
                            w_uq[:, :, NOPE:].reshape(QL, MH * ROPE)], axis=1).astype(BF16)
    w_ukv = lw["mla_w_ukv"].reshape(KV, MH, NOPE + VD)
    w_uk = w_ukv[:, :, :NOPE].reshape(KV, MH * NOPE).astype(BF16)
    w_uv = w_ukv[:, :, NOPE:].reshape(KV, MH * VD).astype(BF16)
    gq, gk = lw["mla_qn_g"], lw["mla_kn_g"]
    gn = (gq[:NOPE] * gk[:NOPE]).reshape(1, NOPE)
    gr = jnp.tile(gq[NOPE:] * gk[NOPE:], 2).reshape(1, ROPE)

    c_q = zA[:, dm["cq_off"]:dm["cq_off"] + QL]
    q = pl.pallas_call(
        functools.partial(_qprep_kernel, MH=MH, NOPE=NOPE, ROPE=ROPE, scale=scale),
        grid=(T // tm,),
        in_specs=[
            pl.BlockSpec((tm, QL), lambda i: (i, 0)),
            pl.BlockSpec((1, QL), lambda i: (0, 0)),
            pl.BlockSpec((QL, MH * QK), lambda i: (0, 0)),
            pl.BlockSpec((tm, MH * ROPE), lambda i: (i, 0)),
            pl.BlockSpec((tm, MH * ROPE), lambda i: (i, 0)),
            pl.BlockSpec((tm, MH * ROPE), lambda i: (i, 0)),
            pl.BlockSpec((1, NOPE), lambda i: (0, 0)),
            pl.BlockSpec((1, ROPE), lambda i: (0, 0)),
        ],
        out_specs=pl.BlockSpec((MH, tm, QK), lambda i: (0, i, 0)),
        out_shape=jax.ShapeDtypeStruct((MH, T, QK), BF16),
        compiler_params=_cp("parallel"),
        name="mla_q_prep",
    )(c_q, lw["mla_qnorm_g"].reshape(1, QL), w_uq, tabs_q[0], tabs_q[1], tabs_q[2], gn, gr)

    kvb = dm["kv_off"] // KV
    kpb = dm["kpe_off"] // (2 * ROPE)
    rows, k, v = pl.pallas_call(
        functools.partial(_kvprep_kernel, MH=MH, NOPE=NOPE, ROPE=ROPE, KV=KV),
        grid=(T // tm,),
        in_specs=[
            pl.BlockSpec((tm, KV), lambda i: (i, kvb)),
            pl.BlockSpec((tm, 2 * ROPE), lambda i: (i, kpb)),
            pl.BlockSpec((1, KV), lambda i: (0, 0)),
            pl.BlockSpec((KV, MH * NOPE), lambda i: (0, 0)),
            pl.BlockSpec((KV, MH * VD), lambda i: (0, 0)),
            pl.BlockSpec((tm, 2 * ROPE), lambda i: (i, 0)),
            pl.BlockSpec((tm, 2 * ROPE), lambda i: (i, 0)),
            pl.BlockSpec((tm, 2 * ROPE), lambda i: (i, 0)),
        ],
        out_specs=[
            pl.BlockSpec((tm, ROW), lambda i: (i, 0)),
            pl.BlockSpec((MH, tm, QK), lambda i: (0, i, 0)),
            pl.BlockSpec((tm, MH * VD), lambda i: (i, 0)),
        ],
        out_shape=[jax.ShapeDtypeStruct((T, ROW), F32),
                   jax.ShapeDtypeStruct((MH, T, QK), BF16),
                   jax.ShapeDtypeStruct((T, MH * VD), BF16)],
        compiler_params=_cp("parallel"),
        name="mla_kv_prep",
    )(zA, zA, lw["mla_kvnorm_g"].reshape(1, KV), w_uk, w_uv, tabs_k[0], tabs_k[1], tabs_k[2])

    TQ = _tile(S, 256)
    nq = S // TQ
    o_p = pl.pallas_call(
        functools.partial(_attn_prompt_kernel, TQ=TQ, S=S),
        grid=(B, MH, nq),
        in_specs=[
            pl.BlockSpec((1, TQ, QK), lambda b, h, i: (h, b * nq + i, 0)),
            pl.BlockSpec((1, S, QK), lambda b, h, i: (h, b, 0)),
            pl.BlockSpec((S, VD), lambda b, h, i: (b, h)),
        ],
        out_specs=pl.BlockSpec((TQ, VD), lambda b, h, i: (b * nq + i, h)),
        out_shape=jax.ShapeDtypeStruct((Tp, MH * VD), BF16),
        compiler_params=_cp("parallel", "parallel", "parallel"),
        name="mla_attn_prompt",
    )(q, k, v)

    PAGE = cache.shape[2]
    NP = page_table.shape[1]
    PP = _tile(NP, 8, 1)
    NQ = LD * MH
    qs = q[:, Tp:].reshape(MH, Bd, LD, QK).transpose(1, 2, 0, 3).reshape(Bd, NQ, QK)
    head_of_row = jnp.arange(NQ) % MH
    eye = (head_of_row[:, None] == jnp.arange(MH)[None, :])
    qn_bd = jnp.where(eye[None, :, :, None], qs[:, :, None, :NOPE], 0).reshape(Bd, NQ, MH * NOPE)
    qr_s = qs[:, :, NOPE:]
    ind = jnp.repeat(eye, NOPE, axis=1).astype(BF16)
    self_rows = jnp.pad(rows[Tp:].reshape(Bd, LD, ROW), ((0, 0), (0, PAGE - LD), (0, 0)))
    page_specs = [
        pl.BlockSpec((1, 1, PAGE, ROW), functools.partial(
            lambda b, s, pt, i: (layer, pt[b, s * PP + i], 0, 0), i=i))
        for i in range(PP)
    ]
    o_s = pl.pallas_call(
        functools.partial(_attn_sample_kernel, PP=PP, MH=MH, NOPE=NOPE, ROPE=ROPE, KV=KV, LD=LD,
                          NP_STEPS=NP // PP),
        grid_spec=pltpu.PrefetchScalarGridSpec(
            num_scalar_prefetch=1,
            grid=(Bd, NP // PP),
            in_specs=[
                pl.BlockSpec((1, NQ, MH * NOPE), lambda b, s, pt: (b, 0, 0)),
                pl.BlockSpec((1, NQ, ROPE), lambda b, s, pt: (b, 0, 0)),
                pl.BlockSpec((1, PAGE, ROW), lambda b, s, pt: (b, 0, 0)),
                *page_specs,
                pl.BlockSpec((KV, MH * NOPE), lambda b, s, pt: (0, 0)),
                pl.BlockSpec((KV, MH * VD), lambda b, s, pt: (0, 0)),
                pl.BlockSpec((NQ, MH * NOPE), lambda b, s, pt: (0, 0)),
            ],
            out_specs=pl.BlockSpec((1, LD, MH * VD), lambda b, s, pt: (b, 0, 0)),
            scratch_shapes=[pltpu.VMEM((NQ, 1), F32), pltpu.VMEM((NQ, 1), F32), pltpu.VMEM((NQ, KV), F32)],
        ),
        out_shape=jax.ShapeDtypeStruct((Bd, LD, MH * VD), BF16),
        compiler_params=_cp("parallel", "arbitrary"),
        name="mla_attn_sample",
    )(page_table, qn_bd, qr_s, self_rows, *([cache] * PP), w_uk, w_uv, ind)
    o_m = jnp.concatenate([o_p, o_s.reshape(Bd * LD, MH * VD)], axis=0)
    return o_m, rows


def _merge_kernel(or_ref, op_ref, om_ref, wr_ref, wp_ref, wm_ref, g0_ref, g1_ref, g2_ref, o_ref):
    acc = jax.nn.sigmoid(g0_ref[...]) * _dot(or_ref[...], wr_ref[...])
    acc = acc + jax.nn.sigmoid(g1_ref[...]) * _dot(op_ref[...], wp_ref[...])
    acc = acc + jax.nn.sigmoid(g2_ref[...]) * _dot(om_ref[...], wm_ref[...])
    o_ref[...] = acc.astype(o_ref.dtype)


def _merge(o_r, o_p, o_m, w_r, w_p, w_m, gl, dm):
    T, D = dm["T"], dm["D"]
    tm = _tile(T, 512)
    tn = _tile(D, 1024, LANES)
    nj = D // tn
    branch = lambda w: pl.BlockSpec((tm, w.shape[0]), lambda j, i: (i, 0))
    weight = lambda w: pl.BlockSpec((w.shape[0], tn), lambda j, i: (0, j))
    gate = lambda n: pl.BlockSpec((tm, tn), lambda j, i: (i, n * nj + j))
    return pl.pallas_call(
        _merge_kernel,
        grid=(nj, T // tm),
        in_specs=[branch(w_r), branch(w_p), branch(w_m), weight(w_r), weight(w_p), weight(w_m),
                  gate(0), gate(1), gate(2)],
        out_specs=pl.BlockSpec((tm, tn), lambda j, i: (i, j)),
        out_shape=jax.ShapeDtypeStruct((T, D), BF16),
        compiler_params=_cp("parallel", "parallel"),
        name="gated_merge",
    )(o_r, o_p, o_m, w_r, w_p, w_m, gl, gl, gl)


def _router_kernel(x_ref, g_ref, w_ref, b_ref, h_ref, r_ref):
    x = x_ref[...]
    h = x * lax.rsqrt(jnp.mean(x * x, axis=-1, keepdims=True) + EPS) * g_ref[...]
    h_ref[...] = h
    logits = jnp.dot(h, w_ref[...], preferred_element_type=F32, precision=lax.Precision.HIGHEST) + b_ref[...]
    lane = lax.broadcasted_iota(I32, logits.shape, 1)
    big = jnp.int32(1 << 20)
    gl = jnp.where(lane < N_GROUPS, logits, -jnp.inf)
    gmax = jnp.max(gl, axis=-1, keepdims=True)
    g_idx = jnp.min(jnp.where(gl == gmax, lane, big), axis=-1, keepdims=True)
    g_w = 1.0 / jnp.sum(jnp.exp(gl - gmax), axis=-1, keepdims=True)
    lo = N_GROUPS + g_idx * EXPERTS_PER_GROUP
    sel = (lane >= lo) & (lane < lo + EXPERTS_PER_GROUP)
    el = jnp.where(sel, logits, -jnp.inf)
    pe = jnp.exp(el - jnp.max(el, axis=-1, keepdims=True))
    p = pe / jnp.sum(pe, axis=-1, keepdims=True)
    p = jnp.where(sel, p, -1.0)
    p1 = jnp.max(p, axis=-1, keepdims=True)
    i1 = jnp.min(jnp.where(p == p1, lane, big), axis=-1, keepdims=True)
    prest = jnp.where(lane == i1, -1.0, p)
    p2 = jnp.max(prest, axis=-1, keepdims=True)
    i2 = jnp.min(jnp.where(prest == p2, lane, big), axis=-1, keepdims=True)
    den = p1 + p2
    w1 = g_w * p1 / den
    w2 = g_w * p2 / den
    out = jnp.where(lane == 0, (i1 - N_GROUPS).astype(F32),
                    jnp.where(lane == 1, (i2 - N_GROUPS).astype(F32),
                              jnp.where(lane == 2, w1, jnp.where(lane == 3, w2, 0.0))))
    r_ref[...] = out


def _moe_up_kernel(te_ref, tok_ref, nu_ref, h_hbm, wg_ref, wu_ref, a_ref, xbuf, sem, *, TM):
    i = pl.program_id(0)

    def row_copy(r, tok):
        return pltpu.make_async_copy(h_hbm.at[pl.ds(tok, 1)], xbuf.at[pl.ds(r, 1)], sem)

    @pl.when(i < nu_ref[0])
    def _():
        def issue(r, c):
            row_copy(r, tok_ref[i * TM + r]).start()
            return c

        lax.fori_loop(0, TM, issue, 0)

        def drain(r, c):
            row_copy(r, 0).wait()
            return c

        lax.fori_loop(0, TM, drain, 0)
        x = xbuf[...].astype(BF16)
        g = _dot(x, wg_ref[0])
        u = _dot(x, wu_ref[0])
        a_ref[...] = (g * jax.nn.sigmoid(g) * u).astype(a_ref.dtype)

    @pl.when(i >= nu_ref[0])
    def _():
        a_ref[...] = jnp.zeros_like(a_ref)


def _moe_down_kernel(te_ref, nu_ref, a_ref, wd_ref, rw_ref, y_ref):
    i = pl.program_id(0)

    @pl.when(i < nu_ref[0])
    def _():
        y_ref[...] = _dot(a_ref[...], wd_ref[0]) * rw_ref[...]

    @pl.when(i >= nu_ref[0])
    def _():
        y_ref[...] = jnp.zeros_like(y_ref)


def _moe_combine_kernel(p1_ref, p2_ref, x_ref, y_hbm, o_ref, b1, b2, sem, *, TM):
    i = pl.program_id(0)

    def row_copy(src_row, dst, r):
        return pltpu.make_async_copy(y_hbm.at[pl.ds(src_row, 1)], dst.at[pl.ds(r, 1)], sem)

    def issue(r, c):
        row_copy(p1_ref[i * TM + r], b1, r).start()
        row_copy(p2_ref[i * TM + r], b2, r).start()
        return c

    lax.fori_loop(0, TM, issue, 0)

    def drain(r, c):
        row_copy(0, b1, r).wait()
        row_copy(0, b2, r).wait()
        return c

    lax.fori_loop(0, TM, drain, 0)
    o_ref[...] = x_ref[...] + (b1[...] + b2[...])


def _moe(x1, norm_g, wr, br, wg, wu, wd, dm):
    T, D = dm["T"], dm["D"]
    E, DE = wg.shape[0], wg.shape[2]
    tm = _tile(T, 256)
    h2, route = pl.pallas_call(
        _router_kernel,
        grid=(T // tm,),
        in_specs=[
            pl.BlockSpec((tm, D), lambda i: (i, 0)),
            pl.BlockSpec((1, D), lambda i: (0, 0)),
            pl.BlockSpec((D, ROUTE_LANES), lambda i: (0, 0)),
            pl.BlockSpec((1, ROUTE_LANES), lambda i: (0, 0)),
        ],
        out_specs=[pl.BlockSpec((tm, D), lambda i: (i, 0)), pl.BlockSpec((tm, ROUTE_LANES), lambda i: (i, 0))],
        out_shape=[jax.ShapeDtypeStruct((T, D), F32), jax.ShapeDtypeStruct((T, ROUTE_LANES), F32)],
        compiler_params=_cp("parallel"),
        name="moe_router",
    )(x1, norm_g.reshape(1, D), wr, br)

    TM = MOE_TM
    n_tiles = pl.cdiv(2 * T, TM) + E
    P = n_tiles * TM
    ids = route[:, :2].astype(I32).reshape(-1)
    wts = route[:, 2:4].reshape(-1)
    onehot = (ids[:, None] == jnp.arange(E, dtype=I32)[None, :]).astype(I32)
    csum = jnp.cumsum(onehot, axis=0)
    rank = jnp.sum(jnp.where(onehot > 0, csum, 0), axis=1) - 1
    counts = csum[-1]
    padded = ((counts + TM - 1) // TM) * TM
    ends = jnp.cumsum(padded)
    starts = ends - padded
    pos = (starts[ids] + rank).astype(I32)
    row_tok = jnp.zeros((P,), I32).at[pos].set(jnp.arange(2 * T, dtype=I32) // 2)
    row_w = jnp.zeros((P,), F32).at[pos].set(wts).reshape(P, 1)
    tile_e = jnp.minimum(jnp.searchsorted(ends, jnp.arange(n_tiles, dtype=I32) * TM, side="right"),
                         E - 1).astype(I32)
    n_used = (ends[-1] // TM).astype(I32).reshape(1)
    pos2 = pos.reshape(T, 2)

    act = pl.pallas_call(
        functools.partial(_moe_up_kernel, TM=TM),
        grid_spec=pltpu.PrefetchScalarGridSpec(
            num_scalar_prefetch=3,
            grid=(n_tiles,),
            in_specs=[
                pl.BlockSpec(memory_space=pl.ANY),
                pl.BlockSpec((1, D, DE), lambda i, te, tok, nu: (te[i], 0, 0)),
                pl.BlockSpec((1, D, DE), lambda i, te, tok, nu: (te[i], 0, 0)),
            ],
            out_specs=pl.BlockSpec((TM, DE), lambda i, te, tok, nu: (i, 0)),
            scratch_shapes=[pltpu.VMEM((TM, D), F32), pltpu.SemaphoreType.DMA(())],
        ),
        out_shape=jax.ShapeDtypeStruct((P, DE), BF16),
        compiler_params=_cp("arbitrary"),
        name="moe_gather_up",
    )(tile_e, row_tok, n_used, h2, wg, wu)

    ys = pl.pallas_call(
        _moe_down_kernel,
        grid_spec=pltpu.PrefetchScalarGridSpec(
            num_scalar_prefetch=2,
            grid=(n_tiles,),
            in_specs=[
                pl.BlockSpec((TM, DE), lambda i, te, nu: (i, 0)),
                pl.BlockSpec((1, DE, D), lambda i, te, nu: (te[i], 0, 0)),
                pl.BlockSpec((TM, 1), lambda i, te, nu: (i, 0)),
            ],
            out_specs=pl.BlockSpec((TM, D), lambda i, te, nu: (i, 0)),
        ),
        out_shape=jax.ShapeDtypeStruct((P, D), F32),
        compiler_params=_cp("arbitrary"),
        name="moe_down",
    )(tile_e, n_used, act, wd, row_w)

    tc = _tile(T, 256)
    return pl.pallas_call(
        functools.partial(_moe_combine_kernel, TM=tc),
        grid_spec=pltpu.PrefetchScalarGridSpec(
            num_scalar_prefetch=2,
            grid=(T // tc,),
            in_specs=[
                pl.BlockSpec((tc, D), lambda i, p1, p2: (i, 0)),
                pl.BlockSpec(memory_space=pl.ANY),
            ],
            out_specs=pl.BlockSpec((tc, D), lambda i, p1, p2: (i, 0)),
            scratch_shapes=[pltpu.VMEM((tc, D), F32), pltpu.VMEM((tc, D), F32), pltpu.SemaphoreType.DMA(())],
        ),
        out_shape=jax.ShapeDtypeStruct((T, D), F32),
        compiler_params=_cp("arbitrary"),
        name="moe_combine",
    )(pos2[:, 0], pos2[:, 1], x1, ys)


def _rope_cos_sin(pos, half):
    inv = ROPE_BASE ** (-jnp.arange(half, dtype=F32) / half)
    ang = pos.astype(F32)[:, None] * inv[None, :]
    return jnp.cos(ang), jnp.sin(ang)


def _ret_tables(pos, dk):
    cos, sin = _rope_cos_sin(pos, dk // 2)
    return jnp.concatenate([cos, cos], axis=1), jnp.concatenate([-sin, sin], axis=1)


def _mla_tables(pos, rope, reps, pad):
    cos, sin = _rope_cos_sin(pos, rope // 2)
    z = jnp.zeros_like(sin)
    tabs = (jnp.concatenate([cos, cos], axis=1), jnp.concatenate([-sin, z], axis=1),
            jnp.concatenate([z, sin], axis=1))
    if pad:
        return tuple(jnp.pad(t, ((0, 0), (0, pad))) for t in tabs)
    return tuple(jnp.tile(t, (1, reps)) for t in tabs)


def kernel(x_prompt, x_sample, cache_mla, page_table, state_ret, state_pool, norm1_g, w_in, ret_gn_g, pool_w, pool_scale, mla_qnorm_g, mla_w_uq, mla_kvnorm_g, mla_w_ukv, mla_qn_g, mla_kn_g, w_br_ret, w_br_pool, w_br_mla, w_out, norm2_g, router_g_w, router_g_b, router_e_w, router_e_b, exp_w_gate, exp_w_up, exp_w_down):
    B, S, D = x_prompt.shape
    Bd, LD, _ = x_sample.shape
    DEPTH = w_in.shape[0]
    H, DK, DV = state_ret.shape[2:]
    PB, PW = state_pool.shape[2:]
    QL = mla_qnorm_g.shape[1]
    KV = mla_kvnorm_g.shape[1]
    ROW = cache_mla.shape[3]
    ROPE = ROW - KV
    NOPE = mla_qn_g.shape[1] - ROPE // 2
    MH = mla_w_uq.shape[2] // (NOPE + ROPE)
    VD = mla_w_ukv.shape[2] // MH - NOPE
    P0 = page_table.shape[1] * cache_mla.shape[2]
    E = exp_w_gate.shape[1]
    assert E == N_GROUPS * EXPERTS_PER_GROUP and 2 * ROPE == LANES
    Tp, Ts = B * S, Bd * LD
    T = Tp + Ts
    u_off = 2 * H * DK + 2 * H * DV
    cq_off = u_off + PW
    kv_off = cq_off + QL
    kpe_off = kv_off + KV
    NA = kpe_off + ROPE
    NA_pad = -(-(NA + ROPE) // LANES) * LANES
    assert w_in.shape[2] == NA + 3 * D and kpe_off % LANES == 0 and kv_off % KV == 0 and u_off % PW == 0
    dm = dict(B=B, S=S, D=D, Bd=Bd, LD=LD, H=H, DK=DK, DV=DV, PB=PB, PW=PW, QL=QL, KV=KV, ROPE=ROPE,
              NOPE=NOPE, MH=MH, VD=VD, P0=P0, Tp=Tp, T=T, u_off=u_off, cq_off=cq_off, kv_off=kv_off,
              kpe_off=kpe_off)

    pos_p = jnp.arange(S)
    pos_s = P0 + jnp.arange(LD)
    pos_all = jnp.concatenate([jnp.tile(pos_p, B), jnp.tile(pos_s, Bd)])
    nb_ret = _tile(Bd, 8, 1)
    tabs_ret_p = _ret_tables(pos_p, DK)
    tabs_ret_s = _ret_tables(jnp.tile(pos_s, nb_ret), DK)
    tabs_q = _mla_tables(pos_all, ROPE, MH, 0)
    tabs_k = _mla_tables(pos_all, ROPE, 1, ROPE)
    lg = jnp.log1p(-jnp.exp2(-5.0 - jnp.arange(H, dtype=F32)))

    x = jnp.concatenate([x_prompt.reshape(Tp, D), x_sample.reshape(Ts, D)], axis=0)
    tmA = _tile(T, 512)
    tnA = _tile(NA_pad, 1024, LANES)
    tnD = _tile(D, 1024, LANES)
    rows_p, rows_s, ret_p, ret_s, pool_p, pool_s = [], [], [], [], [], []
    for l in range(DEPTH):
        lw = {"mla_qnorm_g": mla_qnorm_g[l], "mla_w_uq": mla_w_uq[l], "mla_kvnorm_g": mla_kvnorm_g[l],
              "mla_w_ukv": mla_w_ukv[l], "mla_qn_g": mla_qn_g[l], "mla_kn_g": mla_kn_g[l]}
        wA = jnp.pad(w_in[l, :, :NA].astype(BF16), ((0, 0), (0, NA_pad - NA)))
        wG = w_in[l, :, NA:].astype(BF16)
        h = _rmsnorm(x, norm1_g[l], BF16)
        zA = _matmul(h, wA, tm=tmA, tn=tnA, name="in_proj_main")
        gl = _matmul(h, wG, tm=tmA, tn=tnD, name="in_proj_gates")
        o_r, s_p, s_s = _retention(zA, lg, ret_gn_g[l], tabs_ret_p, tabs_ret_s, state_ret[l], dm)
        o_p, b_p, b_s = _pooling(zA, pool_w[l], pool_scale[l], state_pool[l], dm)
        o_m, rows = _mla(zA, lw, tabs_q, tabs_k, cache_mla, l, page_table, dm)
        merged = _merge(o_r, o_p, o_m, w_br_ret[l].astype(BF16), w_br_pool[l].astype(BF16),
                        w_br_mla[l].astype(BF16), gl, dm)
        x1 = _matmul(merged, w_out[l].astype(BF16), tm=tmA, tn=tnD, res=x, name="out_proj")
        wr = jnp.pad(jnp.concatenate([router_g_w[l], router_e_w[l]], axis=1),
                     ((0, 0), (0, ROUTE_LANES - N_GROUPS - E)))
        br = jnp.pad(jnp.concatenate([router_g_b[l], router_e_b[l]]), (0, ROUTE_LANES - N_GROUPS - E))
        x = _moe(x1, norm2_g[l], wr, br.reshape(1, ROUTE_LANES), exp_w_gate[l].astype(BF16),
                 exp_w_up[l].astype(BF16), exp_w_down[l].astype(BF16), dm)
        rows_p.append(rows[:Tp].reshape(B, S, ROW))
        rows_s.append(rows[Tp:].reshape(Bd, LD, ROW))
        ret_p.append(s_p)
        ret_s.append(s_s)
        pool_p.append(b_p)
        pool_s.append(b_s)
    return (x[:Tp].reshape(B, S, D), x[Tp:].reshape(Bd, LD, D), jnp.stack(rows_p), jnp.stack(rows_s),
            jnp.stack(ret_p), jnp.stack(ret_s), jnp.stack(pool_p), jnp.stack(pool_s))
```

```python
import functools

import jax
import jax.numpy as jnp
from jax import lax
from jax.experimental import pallas as pl
from jax.experimental.pallas import tpu as pltpu

F32 = jnp.float32
BF16 = jnp.bfloat16
I32 = jnp.int32

EPS = 1e-6
ROPE_BASE = 10000.0
NEG_INF = -1e30
POOL_WINDOWS = (2, 4, 8, 16)
N_GROUPS = 4
EXPERTS_PER_GROUP = 4
LANES = 128
VMEM_LIMIT = 56 * 1024 * 1024
ROUTE_LANES = 128
MOE_TM = 256


def _cp(*sem):
    return pltpu.CompilerParams(dimension_semantics=sem, vmem_limit_bytes=VMEM_LIMIT)


def _tile(n, pref, mult=8):
    for t in range(min(pref, n), 0, -1):
        if n % t == 0 and t % mult == 0:
            return t
    return n


def _dot(a, b):
    return jnp.dot(a, b, preferred_element_type=F32)


def _dot_nt(a, b):
    return lax.dot_general(a, b, (((1,), (1,)), ((), ())), preferred_element_type=F32)


def _idiv(x, d):
    return jnp.floor((x.astype(F32) + 0.5) * (1.0 / d)).astype(I32)


def _rmsnorm_kernel(x_ref, g_ref, o_ref):
    x = x_ref[...]
    y = x * lax.rsqrt(jnp.mean(x * x, axis=-1, keepdims=True) + EPS)
    o_ref[...] = (y * g_ref[...]).astype(o_ref.dtype)


def _rmsnorm(x, g, out_dtype):
    T, D = x.shape
    tm = _tile(T, 512)
    return pl.pallas_call(
        _rmsnorm_kernel,
        grid=(T // tm,),
        in_specs=[pl.BlockSpec((tm, D), lambda i: (i, 0)), pl.BlockSpec((1, D), lambda i: (0, 0))],
        out_specs=pl.BlockSpec((tm, D), lambda i: (i, 0)),
        out_shape=jax.ShapeDtypeStruct((T, D), out_dtype),
        compiler_params=_cp("parallel"),
        name="rmsnorm",
    )(x, g.reshape(1, D))


def _mm_kernel(a_ref, b_ref, *rest, has_res, b_is_nk):
    if has_res:
        r_ref, o_ref = rest
    else:
        (o_ref,) = rest
    acc = _dot_nt(a_ref[...], b_ref[...]) if b_is_nk else _dot(a_ref[...], b_ref[...])
    if has_res:
        acc = acc + r_ref[...]
    o_ref[...] = acc.astype(o_ref.dtype)


def _matmul(a, b, *, tm, tn, res=None, out_dtype=F32, b_is_nk=False, name="matmul"):
    M, K = a.shape
    N = b.shape[0] if b_is_nk else b.shape[1]
    b_spec = pl.BlockSpec((tn, K), lambda j, i: (j, 0)) if b_is_nk else pl.BlockSpec((K, tn), lambda j, i: (0, j))
    in_specs = [pl.BlockSpec((tm, K), lambda j, i: (i, 0)), b_spec]
    args = [a, b]
    if res is not None:
        in_specs.append(pl.BlockSpec((tm, tn), lambda j, i: (i, j)))
        args.append(res)
    return pl.pallas_call(
        functools.partial(_mm_kernel, has_res=res is not None, b_is_nk=b_is_nk),
        grid=(N // tn, M // tm),
        in_specs=in_specs,
        out_specs=pl.BlockSpec((tm, tn), lambda j, i: (i, j)),
        out_shape=jax.ShapeDtypeStruct((M, N), out_dtype),
        compiler_params=_cp("parallel", "parallel"),
        name=name,
    )(*args)


def _ret_core(q, k, v, s_prev, decay, qdec):
    qb = q.astype(BF16)
    inner = _dot_nt(qb, k.astype(BF16)) * decay
    o = _dot(inner.astype(BF16), v.astype(BF16))
    return o, qb


def _ret_epilogue(o, gn, gate):
    mu = jnp.mean(o, axis=-1, keepdims=True)
    d = o - mu
    var = jnp.mean(d * d, axis=-1, keepdims=True)
    y = d * lax.rsqrt(var + EPS) * gn
    return y * (gate * jax.nn.sigmoid(gate))


def _ret_prompt_kernel(lg_ref, q_ref, k_ref, v_ref, gr_ref, cos_ref, sin_ref, gn_ref,
                       o_ref, sfin_ref, st_ref, *, C, DK):
    h = pl.program_id(1)
    c = pl.program_id(2)
    lg = lg_ref[h]

    @pl.when(c == 0)
    def _():
        st_ref[...] = jnp.zeros_like(st_ref)

    cos = cos_ref[...]
    sin = sin_ref[...]
    q = q_ref[...]
    k = k_ref[...]
    q = q * cos + pltpu.roll(q, DK // 2, 1) * sin
    k = (k * cos + pltpu.roll(k, DK // 2, 1) * sin) * (DK ** -0.5)
    v = v_ref[...]
    r = lax.broadcasted_iota(I32, (C, 1), 0)
    col = lax.broadcasted_iota(I32, (1, C), 1)
    diff = r - col
    decay = jnp.where(diff >= 0, jnp.exp(lg * jnp.maximum(diff, 0).astype(F32)), 0.0)
    s_prev = st_ref[...]
    o, qb = _ret_core(q, k, v, s_prev, decay, None)
    o = o + _dot(qb, s_prev.astype(BF16)) * jnp.exp(lg * (r + 1).astype(F32))
    kd = k * jnp.exp(lg * (C - 1 - r).astype(F32))
    s_new = jnp.exp(lg * C) * s_prev + _dot(kd.T.astype(BF16), v.astype(BF16))
    st_ref[...] = s_new
    sfin_ref[0, 0] = s_new
    o_ref[...] = _ret_epilogue(o, gn_ref[...], gr_ref[...]).astype(o_ref.dtype)


def _ret_sample_kernel(lg_ref, q_ref, k_ref, v_ref, gr_ref, cos_ref, sin_ref, gn_ref, s_ref,
                       o_ref, snew_ref, *, NB, LD, DK):
    h = pl.program_id(1)
    lg = lg_ref[h]
    L = NB * LD
    cos = cos_ref[...]
    sin = sin_ref[...]
    q = q_ref[...]
    k = k_ref[...]
    q = q * cos + pltpu.roll(q, DK // 2, 1) * sin
    k = (k * cos + pltpu.roll(k, DK // 2, 1) * sin) * (DK ** -0.5)
    v = v_ref[...]
    r = lax.broadcasted_iota(I32, (L, 1), 0)
    col = lax.broadcasted_iota(I32, (1, L), 1)
    rb = _idiv(r, LD)
    rt = r - rb * LD
    same = rb == _idiv(col, LD)
    diff = r - col
    decay = jnp.where(same & (diff >= 0), jnp.exp(lg * jnp.maximum(diff, 0).astype(F32)), 0.0)
    o, qb = _ret_core(q, k, v, None, decay, None)
    qdec = jnp.exp(lg * (rt + 1).astype(F32))
    kd = k * jnp.exp(lg * (LD - 1 - rt).astype(F32))
    vb = v.astype(BF16)
    sdec = jnp.exp(lg * LD)
    for s in range(NB):
        mine = rb == s
        s_prev = s_ref[s, 0]
        o = o + jnp.where(mine, _dot(qb, s_prev.astype(BF16)) * qdec, 0.0)
        kds = jnp.where(mine, kd, 0.0)
        snew_ref[s, 0] = sdec * s_prev + _dot(kds.T.astype(BF16), vb)
    o_ref[...] = _ret_epilogue(o, gn_ref[...], gr_ref[...]).astype(o_ref.dtype)


def _retention(zA, lg, gn, tabs_p, tabs_s, state, dm):
    H, DK, DV, B, S, Bd, LD, Tp = dm["H"], dm["DK"], dm["DV"], dm["B"], dm["S"], dm["Bd"], dm["LD"], dm["Tp"]
    C = _tile(S, 256)
    nc = S // C
    gn2 = gn.reshape(1, H * DV)
    o_p, s_p = pl.pallas_call(
        functools.partial(_ret_prompt_kernel, C=C, DK=DK),
        grid_spec=pltpu.PrefetchScalarGridSpec(
            num_scalar_prefetch=1,
            grid=(B, H, nc),
            in_specs=[
                pl.BlockSpec((C, DK), lambda b, h, c, lg: (b * nc + c, h)),
                pl.BlockSpec((C, DK), lambda b, h, c, lg: (b * nc + c, H + h)),
                pl.BlockSpec((C, DV), lambda b, h, c, lg: (b * nc + c, 2 * H + h)),
                pl.BlockSpec((C, DV), lambda b, h, c, lg: (b * nc + c, 3 * H + h)),
                pl.BlockSpec((C, DK), lambda b, h, c, lg: (c, 0)),
                pl.BlockSpec((C, DK), lambda b, h, c, lg: (c, 0)),
                pl.BlockSpec((1, DV), lambda b, h, c, lg: (0, h)),
            ],
            out_specs=[
                pl.BlockSpec((C, DV), lambda b, h, c, lg: (b * nc + c, h)),
                pl.BlockSpec((1, 1, DK, DV), lambda b, h, c, lg: (b, h, 0, 0)),
            ],
            scratch_shapes=[pltpu.VMEM((DK, DV), F32)],
        ),
        out_shape=[jax.ShapeDtypeStruct((Tp, H * DV), BF16),
                   jax.ShapeDtypeStruct((B, H, DK, DV), F32)],
        compiler_params=_cp("parallel", "parallel", "arbitrary"),
        name="retention_prompt",
    )(lg, zA, zA, zA, zA, tabs_p[0], tabs_p[1], gn2)

    NB = _tile(Bd, 8, 1)
    L = NB * LD
    off = Tp // L
    o_s, s_s = pl.pallas_call(
        functools.partial(_ret_sample_kernel, NB=NB, LD=LD, DK=DK),
        grid_spec=pltpu.PrefetchScalarGridSpec(
            num_scalar_prefetch=1,
            grid=(Bd // NB, H),
            in_specs=[
                pl.BlockSpec((L, DK), lambda i, h, lg: (off + i, h)),
                pl.BlockSpec((L, DK), lambda i, h, lg: (off + i, H + h)),
                pl.BlockSpec((L, DV), lambda i, h, lg: (off + i, 2 * H + h)),
                pl.BlockSpec((L, DV), lambda i, h, lg: (off + i, 3 * H + h)),
                pl.BlockSpec((L, DK), lambda i, h, lg: (0, 0)),
                pl.BlockSpec((L, DK), lambda i, h, lg: (0, 0)),
                pl.BlockSpec((1, DV), lambda i, h, lg: (0, h)),
                pl.BlockSpec((NB, 1, DK, DV), lambda i, h, lg: (i, h, 0, 0)),
            ],
            out_specs=[
                pl.BlockSpec((L, DV), lambda i, h, lg: (i, h)),
                pl.BlockSpec((NB, 1, DK, DV), lambda i, h, lg: (i, h, 0, 0)),
            ],
        ),
        out_shape=[jax.ShapeDtypeStruct((Bd * LD, H * DV), BF16),
                   jax.ShapeDtypeStruct((Bd, H, DK, DV), F32)],
        compiler_params=_cp("parallel", "parallel"),
        name="retention_sample",
    )(lg, zA, zA, zA, zA, tabs_s[0], tabs_s[1], gn2, state)
    return jnp.concatenate([o_p, o_s], axis=0), s_p, s_s


def _pool_prompt_kernel(u_ref, halo_ref, w_ref, sc_ref, o_ref, full_ref, *, TL, HALO, GC):
    i = pl.program_id(1)
    u = u_ref[...]
    full_ref[pl.ds(HALO, TL), :] = u
    full_ref[pl.ds(0, HALO), :] = jnp.where(i == 0, 0.0, halo_ref[...])
    pos = i * TL + lax.broadcasted_iota(I32, (TL, 1), 0)
    for g, w in enumerate(POOL_WINDOWS):
        cs = slice(g * GC, (g + 1) * GC)
        win = full_ref[pl.ds(HALO, TL), cs]
        for j in range(1, w):
            win = win + full_ref[pl.ds(HALO - j, TL), cs]
        cnt = jnp.minimum(w, pos + 1).astype(F32)
        pooled = win / cnt - u[:, cs]
        z = _dot(pooled.astype(BF16), w_ref[g])
        o_ref[:, cs] = (z * sc_ref[:, cs]).astype(o_ref.dtype)


def _pool_sample_kernel(u_ref, buf_ref, w_ref, sc_ref, o_ref, nbuf_ref, *, LD, PB, GC, pos0):
    rows = [buf_ref[j] for j in range(PB)] + [u_ref[:, t, :] for t in range(LD)]
    for j in range(PB):
        nbuf_ref[j] = rows[LD + j]
    for g, w in enumerate(POOL_WINDOWS):
        cs = slice(g * GC, (g + 1) * GC)
        pooled = []
        for t in range(LD):
            win = rows[PB + t][:, cs]
            for j in range(1, w):
                win = win + rows[PB + t - j][:, cs]
            cnt = float(min(w, pos0 + t + 1))
            pooled.append(win / cnt - rows[PB + t][:, cs])
        z = _dot(jnp.concatenate(pooled, axis=0).astype(BF16), w_ref[g])
        nb = u_ref.shape[0]
        for t in range(LD):
            o_ref[:, t, cs] = (z[t * nb:(t + 1) * nb] * sc_ref[:, cs]).astype(o_ref.dtype)


def _pooling(zA, pool_w, pool_scale, buf, dm):
    B, S, Bd, LD, Tp, PW, PB = dm["B"], dm["S"], dm["Bd"], dm["LD"], dm["Tp"], dm["PW"], dm["PB"]
    G = len(POOL_WINDOWS)
    GC = PW // G
    HALO = 16
    assert PB < HALO and S % HALO == 0 and max(POOL_WINDOWS) <= HALO
    TL = _tile(S, 512, HALO)
    nt = S // TL
    ucol = dm["u_off"] // PW
    wb = pool_w.astype(BF16)
    sc = pool_scale.reshape(1, PW)
    o_p = pl.pallas_call(
        functools.partial(_pool_prompt_kernel, TL=TL, HALO=HALO, GC=GC),
        grid=(B, nt),
        in_specs=[
            pl.BlockSpec((TL, PW), lambda b, i: (b * nt + i, ucol)),
            pl.BlockSpec((HALO, PW), lambda b, i: (jnp.maximum((b * nt + i) * (TL // HALO) - 1, 0), ucol)),
            pl.BlockSpec((G, GC, GC), lambda b, i: (0, 0, 0)),
            pl.BlockSpec((1, PW), lambda b, i: (0, 0)),
        ],
        out_specs=pl.BlockSpec((TL, PW), lambda b, i: (b * nt + i, 0)),
        out_shape=jax.ShapeDtypeStruct((Tp, PW), BF16),
        scratch_shapes=[pltpu.VMEM((TL + HALO, PW), F32)],
        compiler_params=_cp("parallel", "parallel"),
        name="pool_prompt",
    )(zA, zA, wb, sc)

    u_s = zA[Tp:, dm["u_off"]:dm["u_off"] + PW].reshape(Bd, LD, PW)
    NB = _tile(Bd, 32, 8)
    o_s, nbuf = pl.pallas_call(
        functools.partial(_pool_sample_kernel, LD=LD, PB=PB, GC=GC, pos0=dm["P0"]),
        grid=(Bd // NB,),
        in_specs=[
            pl.BlockSpec((NB, LD, PW), lambda i: (i, 0, 0)),
            pl.BlockSpec((PB, NB, PW), lambda i: (0, i, 0)),
            pl.BlockSpec((G, GC, GC), lambda i: (0, 0, 0)),
            pl.BlockSpec((1, PW), lambda i: (0, 0)),
        ],
        out_specs=[pl.BlockSpec((NB, LD, PW), lambda i: (i, 0, 0)),
                   pl.BlockSpec((PB, NB, PW), lambda i: (0, i, 0))],
        out_shape=[jax.ShapeDtypeStruct((Bd, LD, PW), BF16), jax.ShapeDtypeStruct((PB, Bd, PW), F32)],
        compiler_params=_cp("parallel"),
        name="pool_sample",
    )(u_s, buf, wb, sc)
    pool_p = zA[:Tp, dm["u_off"]:dm["u_off"] + PW].reshape(B, S, PW)[:, S - PB:]
    return jnp.concatenate([o_p, o_s.reshape(Bd * LD, PW)], axis=0), pool_p, nbuf


def _qprep_kernel(cq_ref, g_ref, w_ref, cos_ref, sa_ref, sb_ref, gn_ref, gr_ref, q_ref,
                  *, MH, NOPE, ROPE, scale):
    x = cq_ref[...]
    y = x * lax.rsqrt(jnp.mean(x * x, axis=-1, keepdims=True) + EPS) * g_ref[...]
    qq = _dot(y.astype(BF16), w_ref[...])
    qn = qq[:, :MH * NOPE]
    qr = qq[:, MH * NOPE:]
    W = MH * ROPE
    qr = (qr * cos_ref[...] + pltpu.roll(qr, W - ROPE // 2, 1) * sa_ref[...]
          + pltpu.roll(qr, ROPE // 2, 1) * sb_ref[...])
    gn = gn_ref[...]
    gr = gr_ref[...]
    for h in range(MH):
        a = qn[:, h * NOPE:(h + 1) * NOPE]
        b = qr[:, h * ROPE:(h + 1) * ROPE]
        ss = jnp.sum(a * a, axis=-1, keepdims=True) + jnp.sum(b * b, axis=-1, keepdims=True)
        inv = lax.rsqrt(ss * (1.0 / (NOPE + ROPE)) + EPS) * scale
        q_ref[h, :, :NOPE] = (a * inv * gn).astype(q_ref.dtype)
        q_ref[h, :, NOPE:] = (b * inv * gr).astype(q_ref.dtype)


def _kvprep_kernel(ckv_ref, kpe_ref, g_ref, wuk_ref, wuv_ref, cos_ref, sa_ref, sb_ref,
                   rows_ref, k_ref, v_ref, *, MH, NOPE, ROPE, KV):
    x = ckv_ref[...]
    lat = x * lax.rsqrt(jnp.mean(x * x, axis=-1, keepdims=True) + EPS) * g_ref[...]
    kp = kpe_ref[...]
    W = kp.shape[1]
    kp = (kp * cos_ref[...] + pltpu.roll(kp, W - ROPE // 2, 1) * sa_ref[...]
          + pltpu.roll(kp, ROPE // 2, 1) * sb_ref[...])
    rows_ref[:, :KV] = lat
    rows_ref[:, KV:] = kp[:, :ROPE]
    latb = lat.astype(BF16)
    kn = _dot(latb, wuk_ref[...])
    v_ref[...] = _dot(latb, wuv_ref[...]).astype(v_ref.dtype)
    sspe = jnp.sum(kp * kp, axis=-1, keepdims=True)
    for h in range(MH):
        a = kn[:, h * NOPE:(h + 1) * NOPE]
        ss = jnp.sum(a * a, axis=-1, keepdims=True) + sspe
        inv = lax.rsqrt(ss * (1.0 / (NOPE + ROPE)) + EPS)
        k_ref[h, :, :NOPE] = (a * inv).astype(k_ref.dtype)
        k_ref[h, :, NOPE:] = (kp[:, :ROPE] * inv).astype(k_ref.dtype)


def _attn_prompt_kernel(q_ref, k_ref, v_ref, o_ref, *, TQ):
    i = pl.program_id(2)
    q = q_ref[0]
    qpos = i * TQ + lax.broadcasted_iota(I32, (TQ, 1), 0)

    def body(j, carry):
        m, l, acc = carry
        start = pl.multiple_of(j * TQ, TQ)
        s = _dot_nt(q, k_ref[0, pl.ds(start, TQ), :])
        kpos = start + lax.broadcasted_iota(I32, (1, TQ), 1)
        s = jnp.where(kpos <= qpos, s, NEG_INF)
        m_new = jnp.maximum(m, jnp.max(s, axis=-1, keepdims=True))
        corr = jnp.exp(m - m_new)
        p = jnp.exp(s - m_new)
        l = l * corr + jnp.sum(p, axis=-1, keepdims=True)
        acc = acc * corr + _dot(p.astype(BF16), v_ref[pl.ds(start, TQ), :])
        return m_new, l, acc

    init = (jnp.full((TQ, 1), NEG_INF, F32), jnp.zeros((TQ, 1), F32), jnp.zeros(o_ref.shape, F32))
    _, l, acc = lax.fori_loop(0, i + 1, body, init)
    o_ref[...] = (acc / l).astype(o_ref.dtype)


def _attn_sample_kernel(pt_ref, qn_ref, qr_ref, self_ref, *rest, PP, MH, NOPE, ROPE, KV, LD, NP_STEPS):
    pages = rest[:PP]
    wukt_ref, wuv_ref, o_ref, qc_ref, m_ref, l_ref, acc_ref = rest[PP:]
    step = pl.program_id(1)
    NQ = LD * MH

    def absorb(chunks, mask):
        s_parts, lat_parts = [], []
        for page_t in chunks:
            W = page_t.shape[1]
            pb = page_t.astype(BF16)
            latb = pb[:KV]
            kn = _dot(wukt_ref[...], latb)
            kpe = page_t[KV:]
            ss = jnp.sum((kn * kn).reshape(MH, NOPE, W), axis=1) + jnp.sum(kpe * kpe, axis=0, keepdims=True)
            rinv = lax.rsqrt(ss * (1.0 / (NOPE + ROPE)) + EPS)
            s_parts.append(_dot(qc_ref[...], pb) * jnp.concatenate([rinv] * LD, axis=0))
            lat_parts.append(latb)
        s = s_parts[0] if len(s_parts) == 1 else jnp.concatenate(s_parts, axis=1)
        if mask is not None:
            s = jnp.where(mask, s, NEG_INF)
        m_old = m_ref[...]
        m_new = jnp.maximum(m_old, jnp.max(s, axis=-1, keepdims=True))
        corr = jnp.exp(m_old - m_new)
        p = jnp.exp(s - m_new)
        l_ref[...] = l_ref[...] * corr + jnp.sum(p, axis=-1, keepdims=True)
        pb16 = p.astype(BF16)
        pv, off = None, 0
        for latb in lat_parts:
            W = latb.shape[1]
            part = _dot_nt(pb16[:, off:off + W], latb)
            pv = part if pv is None else pv + part
            off += W
        acc_ref[...] = acc_ref[...] * corr + pv
        m_ref[...] = m_new

    @pl.when(step == 0)
    def _():
        m_ref[...] = jnp.full_like(m_ref, NEG_INF)
        l_ref[...] = jnp.zeros_like(l_ref)
        acc_ref[...] = jnp.zeros_like(acc_ref)
        qc_ref[:, :KV] = _dot(qn_ref[0], wukt_ref[...]).astype(BF16)
        qc_ref[:, KV:] = qr_ref[0]
        page_t = self_ref[0]
        t = _idiv(lax.broadcasted_iota(I32, (NQ, 1), 0), MH)
        j = lax.broadcasted_iota(I32, (1, page_t.shape[1]), 1)
        absorb([page_t], j <= t)

    if PP % 2 == 0:
        absorb([jnp.concatenate([pages[c][0, 0], pages[c + 1][0, 0]], axis=1) for c in range(0, PP, 2)], None)
    else:
        absorb([p[0, 0] for p in pages], None)

    @pl.when(step == NP_STEPS - 1)
    def _():
        o_lat = (acc_ref[...] / l_ref[...]).astype(BF16)
        VD = wuv_ref.shape[1] // MH
        hrow = lax.broadcasted_iota(I32, (MH, MH * VD), 0)
        hcol = _idiv(lax.broadcasted_iota(I32, (MH, MH * VD), 1), VD)
        for t in range(LD):
            full = _dot(o_lat[t * MH:(t + 1) * MH], wuv_ref[...])
            o_ref[0, pl.ds(t, 1), :] = jnp.sum(jnp.where(hrow == hcol, full, 0.0), axis=0,
                                              keepdims=True).astype(o_ref.dtype)


def _mla(zA, lw, tabs_q, tabs_k, cache_t, layer, page_table, dm):
    T, Tp, B, S, Bd, LD = dm["T"], dm["Tp"], dm["B"], dm["S"], dm["Bd"], dm["LD"]
    MH, NOPE, ROPE, KV, QL, VD = dm["MH"], dm["NOPE"], dm["ROPE"], dm["KV"], dm["QL"], dm["VD"]
    QK = NOPE + ROPE
    ROW = KV + ROPE
    scale = QK ** -0.5
    tm = _tile(T, 512)

    w_uq = lw["mla_w_uq"].reshape(QL, MH, QK)
    w_uq = jnp.concatenate([w_uq[:, :, :NOPE].reshape(QL, MH * NOPE),
                            w_uq[:, :, NOPE:].reshape(QL, MH * ROPE)], axis=1).astype(BF16)
    w_ukv = lw["mla_w_ukv"].reshape(KV, MH, NOPE + VD)
    w_uk = w_ukv[:, :, :NOPE].reshape(KV, MH * NOPE).astype(BF16)
    w_uv = w_ukv[:, :, NOPE:].reshape(KV, MH * VD).astype(BF16)
    gq, gk = lw["mla_qn_g"], lw["mla_kn_g"]
    gn = (gq[:NOPE] * gk[:NOPE]).reshape(1, NOPE)
    gr = jnp.tile(gq[NOPE:] * gk[NOPE:], 2).reshape(1, ROPE)

    c_q = zA[:, dm["cq_off"]:dm["cq_off"] + QL]
    q = pl.pallas_call(
        functools.partial(_qprep_kernel, MH=MH, NOPE=NOPE, ROPE=ROPE, scale=scale),
        grid=(T // tm,),
        in_specs=[
            pl.BlockSpec((tm, QL), lambda i: (i, 0)),
            pl.BlockSpec((1, QL), lambda i: (0, 0)),
            pl.BlockSpec((QL, MH * QK), lambda i: (0, 0)),
            pl.BlockSpec((tm, MH * ROPE), lambda i: (i, 0)),
            pl.BlockSpec((tm, MH * ROPE), lambda i: (i, 0)),
            pl.BlockSpec((tm, MH * ROPE), lambda i: (i, 0)),
            pl.BlockSpec((1, NOPE), lambda i: (0, 0)),
            pl.BlockSpec((1, ROPE), lambda i: (0, 0)),
        ],
        out_specs=pl.BlockSpec((MH, tm, QK), lambda i: (0, i, 0)),
        out_shape=jax.ShapeDtypeStruct((MH, T, QK), BF16),
        compiler_params=_cp("parallel"),
        name="mla_q_prep",
    )(c_q, lw["mla_qnorm_g"].reshape(1, QL), w_uq, tabs_q[0], tabs_q[1], tabs_q[2], gn, gr)

    kvb = dm["kv_off"] // KV
    kpb = dm["kpe_off"] // (2 * ROPE)
    rows, k, v = pl.pallas_call(
        functools.partial(_kvprep_kernel, MH=MH, NOPE=NOPE, ROPE=ROPE, KV=KV),
        grid=(T // tm,),
        in_specs=[
            pl.BlockSpec((tm, KV), lambda i: (i, kvb)),
            pl.BlockSpec((tm, 2 * ROPE), lambda i: (i, kpb)),
            pl.BlockSpec((1, KV), lambda i: (0, 0)),
            pl.BlockSpec((KV, MH * NOPE), lambda i: (0, 0)),
            pl.BlockSpec((KV, MH * VD), lambda i: (0, 0)),
            pl.BlockSpec((tm, 2 * ROPE), lambda i: (i, 0)),
            pl.BlockSpec((tm, 2 * ROPE), lambda i: (i, 0)),
            pl.BlockSpec((tm, 2 * ROPE), lambda i: (i, 0)),
        ],
        out_specs=[
            pl.BlockSpec((tm, ROW), lambda i: (i, 0)),
            pl.BlockSpec((MH, tm, QK), lambda i: (0, i, 0)),
            pl.BlockSpec((tm, MH * VD), lambda i: (i, 0)),
        ],
        out_shape=[jax.ShapeDtypeStruct((T, ROW), F32),
                   jax.ShapeDtypeStruct((MH, T, QK), BF16),
                   jax.ShapeDtypeStruct((T, MH * VD), BF16)],
        compiler_params=_cp("parallel"),
        name="mla_kv_prep",
    )(zA, zA, lw["mla_kvnorm_g"].reshape(1, KV), w_uk, w_uv, tabs_k[0], tabs_k[1], tabs_k[2])

    TQ = _tile(S, 512)
    nq = S // TQ
    o_p = pl.pallas_call(
        functools.partial(_attn_prompt_kernel, TQ=TQ),
        grid=(B, MH, nq),
        in_specs=[
            pl.BlockSpec((1, TQ, QK), lambda b, h, i: (h, b * nq + i, 0)),
            pl.BlockSpec((1, S, QK), lambda b, h, i: (h, b, 0)),
            pl.BlockSpec((S, VD), lambda b, h, i: (b, h)),
        ],
        out_specs=pl.BlockSpec((TQ, VD), lambda b, h, i: (b * nq + i, h)),
        out_shape=jax.ShapeDtypeStruct((Tp, MH * VD), BF16),
        compiler_params=_cp("parallel", "parallel", "parallel"),
        name="mla_attn_prompt",
    )(q, k, v)

    PAGE = cache_t.shape[3]
    NP = page_table.shape[1]
    PP = _tile(NP, 8, 1)
    NQ = LD * MH
    qs = q[:, Tp:].reshape(MH, Bd, LD, QK).transpose(1, 2, 0, 3).reshape(Bd, NQ, QK)
    head_of_row = jnp.arange(NQ) % MH
    eye = (head_of_row[:, None] == jnp.arange(MH)[None, :])
    qn_bd = jnp.where(eye[None, :, :, None], qs[:, :, None, :NOPE], 0).reshape(Bd, NQ, MH * NOPE)
    qr_s = qs[:, :, NOPE:]
    self_t = jnp.pad(rows[Tp:].reshape(Bd, LD, ROW).transpose(0, 2, 1), ((0, 0), (0, 0), (0, PAGE - LD)))
    page_specs = [
        pl.BlockSpec((1, 1, ROW, PAGE), functools.partial(
            lambda b, s, pt, i: (layer, pt[b, s * PP + i], 0, 0), i=i))
        for i in range(PP)
    ]
    o_s = pl.pallas_call(
        functools.partial(_attn_sample_kernel, PP=PP, MH=MH, NOPE=NOPE, ROPE=ROPE, KV=KV, LD=LD,
                          NP_STEPS=NP // PP),
        grid_spec=pltpu.PrefetchScalarGridSpec(
            num_scalar_prefetch=1,
            grid=(Bd, NP // PP),
            in_specs=[
                pl.BlockSpec((1, NQ, MH * NOPE), lambda b, s, pt: (b, 0, 0)),
                pl.BlockSpec((1, NQ, ROPE), lambda b, s, pt: (b, 0, 0)),
                pl.BlockSpec((1, ROW, PAGE), lambda b, s, pt: (b, 0, 0)),
                *page_specs,
                pl.BlockSpec((MH * NOPE, KV), lambda b, s, pt: (0, 0)),
                pl.BlockSpec((KV, MH * VD), lambda b, s, pt: (0, 0)),
            ],
            out_specs=pl.BlockSpec((1, LD, MH * VD), lambda b, s, pt: (b, 0, 0)),
            scratch_shapes=[pltpu.VMEM((NQ, ROW), BF16), pltpu.VMEM((NQ, 1), F32), pltpu.VMEM((NQ, 1), F32),
                            pltpu.VMEM((NQ, KV), F32)],
        ),
        out_shape=jax.ShapeDtypeStruct((Bd, LD, MH * VD), BF16),
        compiler_params=_cp("parallel", "arbitrary"),
        name="mla_attn_sample",
    )(page_table, qn_bd, qr_s, self_t, *([cache_t] * PP), w_uk.T, w_uv)
    o_m = jnp.concatenate([o_p, o_s.reshape(Bd * LD, MH * VD)], axis=0)
    return o_m, rows


def _merge_kernel(or_ref, op_ref, om_ref, wr_ref, wp_ref, wm_ref, g0_ref, g1_ref, g2_ref, o_ref):
    acc = jax.nn.sigmoid(g0_ref[...]) * _dot(or_ref[...], wr_ref[...])
    acc = acc + jax.nn.sigmoid(g1_ref[...]) * _dot(op_ref[...], wp_ref[...])
    acc = acc + jax.nn.sigmoid(g2_ref[...]) * _dot(om_ref[...], wm_ref[...])
    o_ref[...] = acc.astype(o_ref.dtype)


def _merge(o_r, o_p, o_m, w_r, w_p, w_m, gl, dm):
    T, D = dm["T"], dm["D"]
    tm = _tile(T, 512)
    tn = _tile(D, 1024, LANES)
    nj = D // tn
    branch = lambda w: pl.BlockSpec((tm, w.shape[0]), lambda j, i: (i, 0))
    weight = lambda w: pl.BlockSpec((w.shape[0], tn), lambda j, i: (0, j))
    gate = lambda n: pl.BlockSpec((tm, tn), lambda j, i: (i, n * nj + j))
    return pl.pallas_call(
        _merge_kernel,
        grid=(nj, T // tm),
        in_specs=[branch(w_r), branch(w_p), branch(w_m), weight(w_r), weight(w_p), weight(w_m),
                  gate(0), gate(1), gate(2)],
        out_specs=pl.BlockSpec((tm, tn), lambda j, i: (i, j)),
        out_shape=jax.ShapeDtypeStruct((T, D), BF16),
        compiler_params=_cp("parallel", "parallel"),
        name="gated_merge",
    )(o_r, o_p, o_m, w_r, w_p, w_m, gl, gl, gl)


def _router_kernel(x_ref, g_ref, w_ref, b_ref, h_ref, r_ref):
    x = x_ref[...]
    h = x * lax.rsqrt(jnp.mean(x * x, axis=-1, keepdims=True) + EPS) * g_ref[...]
    h_ref[...] = h
    logits = jnp.dot(h, w_ref[...], preferred_element_type=F32, precision=lax.Precision.HIGHEST) + b_ref[...]
    lane = lax.broadcasted_iota(I32, logits.shape, 1)
    big = jnp.int32(1 << 20)
    gl = jnp.where(lane < N_GROUPS, logits, -jnp.inf)
    gmax = jnp.max(gl, axis=-1, keepdims=True)
    g_idx = jnp.min(jnp.where(gl == gmax, lane, big), axis=-1, keepdims=True)
    g_w = 1.0 / jnp.sum(jnp.exp(gl - gmax), axis=-1, keepdims=True)
    lo = N_GROUPS + g_idx * EXPERTS_PER_GROUP
    sel = (lane >= lo) & (lane < lo + EXPERTS_PER_GROUP)
    el = jnp.where(sel, logits, -jnp.inf)
    pe = jnp.exp(el - jnp.max(el, axis=-1, keepdims=True))
    p = pe / jnp.sum(pe, axis=-1, keepdims=True)
    p = jnp.where(sel, p, -1.0)
    p1 = jnp.max(p, axis=-1, keepdims=True)
    i1 = jnp.min(jnp.where(p == p1, lane, big), axis=-1, keepdims=True)
    prest = jnp.where(lane == i1, -1.0, p)
    p2 = jnp.max(prest, axis=-1, keepdims=True)
    i2 = jnp.min(jnp.where(prest == p2, lane, big), axis=-1, keepdims=True)
    den = p1 + p2
    w1 = g_w * p1 / den
    w2 = g_w * p2 / den
    out = jnp.where(lane == 0, (i1 - N_GROUPS).astype(F32),
                    jnp.where(lane == 1, (i2 - N_GROUPS).astype(F32),
                              jnp.where(lane == 2, w1, jnp.where(lane == 3, w2, 0.0))))
    r_ref[...] = out


def _moe_up_kernel(te_ref, tok_ref, nu_ref, h_hbm, wg_ref, wu_ref, a_ref, xbuf, sem, *, TM):
    i = pl.program_id(0)
    n_used = nu_ref[0]
    slot = i % 2

    def row_copy(s, r, tok):
        return pltpu.make_async_copy(h_hbm.at[pl.ds(tok, 1)], xbuf.at[s, pl.ds(r, 1)], sem.at[s])

    def gather(tile, s):
        def issue(r, c):
            row_copy(s, r, tok_ref[tile * TM + r]).start()
            return c

        lax.fori_loop(0, TM, issue, 0)

    @pl.when((i == 0) & (n_used > 0))
    def _():
        gather(0, 0)

    @pl.when(i + 1 < n_used)
    def _():
        gather(i + 1, 1 - slot)

    @pl.when(i < n_used)
    def _():
        def drain(r, c):
            row_copy(slot, r, 0).wait()
            return c

        lax.fori_loop(0, TM, drain, 0)
        x = xbuf[slot].astype(BF16)
        g = _dot(x, wg_ref[0])
        u = _dot(x, wu_ref[0])
        a_ref[...] = (g * jax.nn.sigmoid(g) * u).astype(a_ref.dtype)

    @pl.when(i >= n_used)
    def _():
        a_ref[...] = jnp.zeros_like(a_ref)


def _moe_down_kernel(te_ref, nu_ref, a_ref, wd_ref, rw_ref, y_ref):
    i = pl.program_id(0)

    @pl.when(i < nu_ref[0])
    def _():
        y_ref[...] = _dot(a_ref[...], wd_ref[0]) * rw_ref[...]

    @pl.when(i >= nu_ref[0])
    def _():
        y_ref[...] = jnp.zeros_like(y_ref)


def _moe_combine_kernel(p1_ref, p2_ref, x_ref, y_hbm, o_ref, b1, b2, sem, *, TM, NT):
    i = pl.program_id(0)
    slot = i % 2

    def row_copy(s, src_row, dst, r):
        return pltpu.make_async_copy(y_hbm.at[pl.ds(src_row, 1)], dst.at[s, pl.ds(r, 1)], sem.at[s])

    def gather(tile, s):
        def issue(r, c):
            row_copy(s, p1_ref[tile * TM + r], b1, r).start()
            row_copy(s, p2_ref[tile * TM + r], b2, r).start()
            return c

        lax.fori_loop(0, TM, issue, 0)

    @pl.when(i == 0)
    def _():
        gather(0, 0)

    @pl.when(i + 1 < NT)
    def _():
        gather(i + 1, 1 - slot)

    def drain(r, c):
        row_copy(slot, 0, b1, r).wait()
        row_copy(slot, 0, b2, r).wait()
        return c

    lax.fori_loop(0, TM, drain, 0)
    o_ref[...] = x_ref[...] + (b1[slot] + b2[slot])


def _moe(x1, norm_g, wr, br, wg, wu, wd, dm):
    T, D = dm["T"], dm["D"]
    E, DE = wg.shape[0], wg.shape[2]
    tm = _tile(T, 256)
    h2, route = pl.pallas_call(
        _router_kernel,
        grid=(T // tm,),
        in_specs=[
            pl.BlockSpec((tm, D), lambda i: (i, 0)),
            pl.BlockSpec((1, D), lambda i: (0, 0)),
            pl.BlockSpec((D, ROUTE_LANES), lambda i: (0, 0)),
            pl.BlockSpec((1, ROUTE_LANES), lambda i: (0, 0)),
        ],
        out_specs=[pl.BlockSpec((tm, D), lambda i: (i, 0)), pl.BlockSpec((tm, ROUTE_LANES), lambda i: (i, 0))],
        out_shape=[jax.ShapeDtypeStruct((T, D), F32), jax.ShapeDtypeStruct((T, ROUTE_LANES), F32)],
        compiler_params=_cp("parallel"),
        name="moe_router",
    )(x1, norm_g.reshape(1, D), wr, br)

    TM = MOE_TM
    n_tiles = pl.cdiv(2 * T, TM) + E
    P = n_tiles * TM
    ids = route[:, :2].astype(I32).reshape(-1)
    wts = route[:, 2:4].reshape(-1)
    onehot = (ids[:, None] == jnp.arange(E, dtype=I32)[None, :]).astype(I32)
    csum = jnp.cumsum(onehot, axis=0)
    rank = jnp.sum(jnp.where(onehot > 0, csum, 0), axis=1) - 1
    counts = csum[-1]
    padded = ((counts + TM - 1) // TM) * TM
    ends = jnp.cumsum(padded)
    starts = ends - padded
    pos = (starts[ids] + rank).astype(I32)
    row_tok = jnp.zeros((P,), I32).at[pos].set(jnp.arange(2 * T, dtype=I32) // 2)
    row_w = jnp.zeros((P,), F32).at[pos].set(wts).reshape(P, 1)
    tile_e = jnp.minimum(jnp.searchsorted(ends, jnp.arange(n_tiles, dtype=I32) * TM, side="right"),
                         E - 1).astype(I32)
    n_used = (ends[-1] // TM).astype(I32).reshape(1)
    pos2 = pos.reshape(T, 2)

    act = pl.pallas_call(
        functools.partial(_moe_up_kernel, TM=TM),
        grid_spec=pltpu.PrefetchScalarGridSpec(
            num_scalar_prefetch=3,
            grid=(n_tiles,),
            in_specs=[
                pl.BlockSpec(memory_space=pl.ANY),
                pl.BlockSpec((1, D, DE), lambda i, te, tok, nu: (te[i], 0, 0)),
                pl.BlockSpec((1, D, DE), lambda i, te, tok, nu: (te[i], 0, 0)),
            ],
            out_specs=pl.BlockSpec((TM, DE), lambda i, te, tok, nu: (i, 0)),
            scratch_shapes=[pltpu.VMEM((2, TM, D), F32), pltpu.SemaphoreType.DMA((2,))],
        ),
        out_shape=jax.ShapeDtypeStruct((P, DE), BF16),
        compiler_params=_cp("arbitrary"),
        name="moe_gather_up",
    )(tile_e, row_tok, n_used, h2, wg, wu)

    ys = pl.pallas_call(
        _moe_down_kernel,
        grid_spec=pltpu.PrefetchScalarGridSpec(
            num_scalar_prefetch=2,
            grid=(n_tiles,),
            in_specs=[
                pl.BlockSpec((TM, DE), lambda i, te, nu: (i, 0)),
                pl.BlockSpec((1, DE, D), lambda i, te, nu: (te[i], 0, 0)),
                pl.BlockSpec((TM, 1), lambda i, te, nu: (i, 0)),
            ],
            out_specs=pl.BlockSpec((TM, D), lambda i, te, nu: (i, 0)),
        ),
        out_shape=jax.ShapeDtypeStruct((P, D), F32),
        compiler_params=_cp("arbitrary"),
        name="moe_down",
    )(tile_e, n_used, act, wd, row_w)

    tc = _tile(T, 256)
    return pl.pallas_call(
        functools.partial(_moe_combine_kernel, TM=tc, NT=T // tc),
        grid_spec=pltpu.PrefetchScalarGridSpec(
            num_scalar_prefetch=2,
            grid=(T // tc,),
            in_specs=[
                pl.BlockSpec((tc, D), lambda i, p1, p2: (i, 0)),
                pl.BlockSpec(memory_space=pl.ANY),
            ],
            out_specs=pl.BlockSpec((tc, D), lambda i, p1, p2: (i, 0)),
            scratch_shapes=[pltpu.VMEM((2, tc, D), F32), pltpu.VMEM((2, tc, D), F32),
                            pltpu.SemaphoreType.DMA((2,))],
        ),
        out_shape=jax.ShapeDtypeStruct((T, D), F32),
        compiler_params=_cp("arbitrary"),
        name="moe_combine",
    )(pos2[:, 0], pos2[:, 1], x1, ys)


def _rope_cos_sin(pos, half):
    inv = ROPE_BASE ** (-jnp.arange(half, dtype=F32) / half)
    ang = pos.astype(F32)[:, None] * inv[None, :]
    return jnp.cos(ang), jnp.sin(ang)


def _ret_tables(pos, dk):
    cos, sin = _rope_cos_sin(pos, dk // 2)
    return jnp.concatenate([cos, cos], axis=1), jnp.concatenate([-sin, sin], axis=1)


def _mla_tables(pos, rope, reps, pad):
    cos, sin = _rope_cos_sin(pos, rope // 2)
    z = jnp.zeros_like(sin)
    tabs = (jnp.concatenate([cos, cos], axis=1), jnp.concatenate([-sin, z], axis=1),
            jnp.concatenate([z, sin], axis=1))
    if pad:
        return tuple(jnp.pad(t, ((0, 0), (0, pad))) for t in tabs)
    return tuple(jnp.tile(t, (1, reps)) for t in tabs)


def kernel(x_prompt, x_sample, cache_mla, page_table, state_ret, state_pool, norm1_g, w_in, ret_gn_g, pool_w, pool_scale, mla_qnorm_g, mla_w_uq, mla_kvnorm_g, mla_w_ukv, mla_qn_g, mla_kn_g, w_br_ret, w_br_pool, w_br_mla, w_out, norm2_g, router_g_w, router_g_b, router_e_w, router_e_b, exp_w_gate, exp_w_up, exp_w_down):
    B, S, D = x_prompt.shape
    Bd, LD, _ = x_sample.shape
    DEPTH = w_in.shape[0]
    H, DK, DV = state_ret.shape[2:]
    PB, PW = state_pool.shape[2:]
    QL = mla_qnorm_g.shape[1]
    KV = mla_kvnorm_g.shape[1]
    ROW = cache_mla.shape[3]
    ROPE = ROW - KV
    NOPE = mla_qn_g.shape[1] - ROPE // 2
    MH = mla_w_uq.shape[2] // (NOPE + ROPE)
    VD = mla_w_ukv.shape[2] // MH - NOPE
    P0 = page_table.shape[1] * cache_mla.shape[2]
    E = exp_w_gate.shape[1]
    assert E == N_GROUPS * EXPERTS_PER_GROUP and 2 * ROPE == LANES
    Tp, Ts = B * S, Bd * LD
    T = Tp + Ts
    u_off = 2 * H * DK + 2 * H * DV
    cq_off = u_off + PW
    kv_off = cq_off + QL
    kpe_off = kv_off + KV
    NA = kpe_off + ROPE
    NA_pad = -(-(NA + ROPE) // LANES) * LANES
    assert w_in.shape[2] == NA + 3 * D and kpe_off % LANES == 0 and kv_off % KV == 0 and u_off % PW == 0
    dm = dict(B=B, S=S, D=D, Bd=Bd, LD=LD, H=H, DK=DK, DV=DV, PB=PB, PW=PW, QL=QL, KV=KV, ROPE=ROPE,
              NOPE=NOPE, MH=MH, VD=VD, P0=P0, Tp=Tp, T=T, u_off=u_off, cq_off=cq_off, kv_off=kv_off,
              kpe_off=kpe_off)

    pos_p = jnp.arange(S)
    pos_s = P0 + jnp.arange(LD)
    pos_all = jnp.concatenate([jnp.tile(pos_p, B), jnp.tile(pos_s, Bd)])
    nb_ret = _tile(Bd, 8, 1)
    tabs_ret_p = _ret_tables(pos_p, DK)
    tabs_ret_s = _ret_tables(jnp.tile(pos_s, nb_ret), DK)
    tabs_q = _mla_tables(pos_all, ROPE, MH, 0)
    tabs_k = _mla_tables(pos_all, ROPE, 1, ROPE)
    lg = jnp.log1p(-jnp.exp2(-5.0 - jnp.arange(H, dtype=F32)))

    x = jnp.concatenate([x_prompt.reshape(Tp, D), x_sample.reshape(Ts, D)], axis=0)
    tmA = _tile(T, 512)
    tnA = _tile(NA_pad, 1024, LANES)
    tnD = _tile(D, 1024, LANES)
    rows_p, rows_s, ret_p, ret_s, pool_p, pool_s = [], [], [], [], [], []
    cache_t = jnp.swapaxes(cache_mla, 2, 3)
    w_in_t = jnp.swapaxes(w_in, 1, 2)
    state_pool_t = jnp.swapaxes(state_pool, 1, 2)
    for l in range(DEPTH):
        lw = {"mla_qnorm_g": mla_qnorm_g[l], "mla_w_uq": mla_w_uq[l], "mla_kvnorm_g": mla_kvnorm_g[l],
              "mla_w_ukv": mla_w_ukv[l], "mla_qn_g": mla_qn_g[l], "mla_kn_g": mla_kn_g[l]}
        wA = jnp.pad(w_in_t[l, :NA].astype(BF16), ((0, NA_pad - NA), (0, 0)))
        wG = w_in_t[l, NA:].astype(BF16)
        h = _rmsnorm(x, norm1_g[l], BF16)
        zA = _matmul(h, wA, tm=tmA, tn=tnA, b_is_nk=True, name="in_proj_main")
        gl = _matmul(h, wG, tm=tmA, tn=tnD, b_is_nk=True, name="in_proj_gates")
        o_r, s_p, s_s = _retention(zA, lg, ret_gn_g[l], tabs_ret_p, tabs_ret_s, state_ret[l], dm)
        o_p, b_p, b_s = _pooling(zA, pool_w[l], pool_scale[l], state_pool_t[l], dm)
        o_m, rows = _mla(zA, lw, tabs_q, tabs_k, cache_t, l, page_table, dm)
        merged = _merge(o_r, o_p, o_m, w_br_ret[l].astype(BF16), w_br_pool[l].astype(BF16),
                        w_br_mla[l].astype(BF16), gl, dm)
        x1 = _matmul(merged, w_out[l].astype(BF16), tm=tmA, tn=tnD, res=x, name="out_proj")
        wr = jnp.pad(jnp.concatenate([router_g_w[l], router_e_w[l]], axis=1),
                     ((0, 0), (0, ROUTE_LANES - N_GROUPS - E)))
        br = jnp.pad(jnp.concatenate([router_g_b[l], router_e_b[l]]), (0, ROUTE_LANES - N_GROUPS - E))
        x = _moe(x1, norm2_g[l], wr, br.reshape(1, ROUTE_LANES), exp_w_gate[l].astype(BF16),
                 exp_w_up[l].astype(BF16), exp_w_down[l].astype(BF16), dm)
        rows_p.append(rows[:Tp].reshape(B, S, ROW))
        rows_s.append(rows[Tp:].reshape(Bd, LD, ROW))
        ret_p.append(s_p)
        ret_s.append(s_s)
        pool_p.append(b_p)
        pool_s.append(b_s)
    return (x[:Tp].reshape(B, S, D), x[Tp:].reshape(Bd, LD, D), jnp.stack(rows_p), jnp.stack(rows_s),
            jnp.stack(ret_p), jnp.stack(ret_s), jnp.stack(pool_p), jnp.swapaxes(jnp.stack(pool_s), 1, 2))
```

```python
import functools

import jax
import jax.numpy as jnp
from jax import lax
from jax.experimental import pallas as pl
from jax.experimental.pallas import tpu as pltpu

F32 = jnp.float32
BF16 = jnp.bfloat16
I32 = jnp.int32

EPS = 1e-6
ROPE_BASE = 10000.0
NEG_INF = -1e30
POOL_WINDOWS = (2, 4, 8, 16)
N_GROUPS = 4
EXPERTS_PER_GROUP = 4
LANES = 128
VMEM_LIMIT = 56 * 1024 * 1024
ROUTE_LANES = 128
MOE_TM = 256


def _cp(*sem):
    return pltpu.CompilerParams(dimension_semantics=sem, vmem_limit_bytes=VMEM_LIMIT)


def _tile(n, pref, mult=8):
    for t in range(min(pref, n), 0, -1):
        if n % t == 0 and t % mult == 0:
            return t
    return n


def _dot(a, b):
    return jnp.dot(a, b, preferred_element_type=F32)


def _dot_nt(a, b):
    return lax.dot_general(a, b, (((1,), (1,)), ((), ())), preferred_element_type=F32)


def _idiv(x, d):
    return jnp.floor((x.astype(F32) + 0.5) * (1.0 / d)).astype(I32)


def _rmsnorm_kernel(x_ref, g_ref, o_ref):
    x = x_ref[...]
    y = x * lax.rsqrt(jnp.mean(x * x, axis=-1, keepdims=True) + EPS)
    o_ref[...] = (y * g_ref[...]).astype(o_ref.dtype)


def _rmsnorm(x, g, out_dtype):
    T, D = x.shape
    tm = _tile(T, 512)
    return pl.pallas_call(
        _rmsnorm_kernel,
        grid=(T // tm,),
        in_specs=[pl.BlockSpec((tm, D), lambda i: (i, 0)), pl.BlockSpec((1, D), lambda i: (0, 0))],
        out_specs=pl.BlockSpec((tm, D), lambda i: (i, 0)),
        out_shape=jax.ShapeDtypeStruct((T, D), out_dtype),
        compiler_params=_cp("parallel"),
        name="rmsnorm",
    )(x, g.reshape(1, D))


def _mm_kernel(a_ref, b_ref, *rest, has_res, b_is_nk):
    if has_res:
        r_ref, o_ref = rest
    else:
        (o_ref,) = rest
    acc = _dot_nt(a_ref[...], b_ref[0]) if b_is_nk else _dot(a_ref[...], b_ref[0])
    if has_res:
        acc = acc + r_ref[...]
    o_ref[...] = acc.astype(o_ref.dtype)


def _matmul(a, b, layer, *, tm, tn, n=None, res=None, out_dtype=F32, b_is_nk=False, name="matmul"):
    M, K = a.shape
    N = n if n is not None else (b.shape[1] if b_is_nk else b.shape[2])
    if b_is_nk:
        b_spec = pl.BlockSpec((1, tn, K), lambda j, i: (layer, j, 0))
    else:
        b_spec = pl.BlockSpec((1, K, tn), lambda j, i: (layer, 0, j))
    in_specs = [pl.BlockSpec((tm, K), lambda j, i: (i, 0)), b_spec]
    args = [a, b]
    if res is not None:
        in_specs.append(pl.BlockSpec((tm, tn), lambda j, i: (i, j)))
        args.append(res)
    return pl.pallas_call(
        functools.partial(_mm_kernel, has_res=res is not None, b_is_nk=b_is_nk),
        grid=(N // tn, M // tm),
        in_specs=in_specs,
        out_specs=pl.BlockSpec((tm, tn), lambda j, i: (i, j)),
        out_shape=jax.ShapeDtypeStruct((M, N), out_dtype),
        compiler_params=_cp("parallel", "parallel"),
        name=name,
    )(*args)


def _ret_core(q, k, v, s_prev, decay, qdec):
    qb = q.astype(BF16)
    inner = _dot_nt(qb, k.astype(BF16)) * decay
    o = _dot(inner.astype(BF16), v.astype(BF16))
    return o, qb


def _ret_epilogue(o, gn, gate):
    mu = jnp.mean(o, axis=-1, keepdims=True)
    d = o - mu
    var = jnp.mean(d * d, axis=-1, keepdims=True)
    y = d * lax.rsqrt(var + EPS) * gn
    return y * (gate * jax.nn.sigmoid(gate))


def _ret_prompt_kernel(lg_ref, q_ref, k_ref, v_ref, gr_ref, cos_ref, sin_ref, gn_ref,
                       o_ref, sfin_ref, st_ref, *, C, DK):
    h = pl.program_id(1)
    c = pl.program_id(2)
    lg = lg_ref[h]

    @pl.when(c == 0)
    def _():
        st_ref[...] = jnp.zeros_like(st_ref)

    cos = cos_ref[...]
    sin = sin_ref[...]
    q = q_ref[...]
    k = k_ref[...]
    q = q * cos + pltpu.roll(q, DK // 2, 1) * sin
    k = (k * cos + pltpu.roll(k, DK // 2, 1) * sin) * (DK ** -0.5)
    v = v_ref[...]
    r = lax.broadcasted_iota(I32, (C, 1), 0)
    col = lax.broadcasted_iota(I32, (1, C), 1)
    diff = r - col
    decay = jnp.where(diff >= 0, jnp.exp(lg * jnp.maximum(diff, 0).astype(F32)), 0.0)
    s_prev = st_ref[...]
    o, qb = _ret_core(q, k, v, s_prev, decay, None)
    o = o + _dot(qb, s_prev.astype(BF16)) * jnp.exp(lg * (r + 1).astype(F32))
    kd = k * jnp.exp(lg * (C - 1 - r).astype(F32))
    s_new = jnp.exp(lg * C) * s_prev + _dot(kd.T.astype(BF16), v.astype(BF16))
    st_ref[...] = s_new
    sfin_ref[0, 0] = s_new
    o_ref[...] = _ret_epilogue(o, gn_ref[...], gr_ref[...]).astype(o_ref.dtype)


def _ret_sample_kernel(lg_ref, q_ref, k_ref, v_ref, gr_ref, cos_ref, sin_ref, gn_ref, s_ref,
                       o_ref, snew_ref, *, NB, LD, DK):
    h = pl.program_id(1)
    lg = lg_ref[h]
    L = NB * LD
    cos = cos_ref[...]
    sin = sin_ref[...]
    q = q_ref[...]
    k = k_ref[...]
    q = q * cos + pltpu.roll(q, DK // 2, 1) * sin
    k = (k * cos + pltpu.roll(k, DK // 2, 1) * sin) * (DK ** -0.5)
    v = v_ref[...]
    r = lax.broadcasted_iota(I32, (L, 1), 0)
    col = lax.broadcasted_iota(I32, (1, L), 1)
    rb = _idiv(r, LD)
    rt = r - rb * LD
    same = rb == _idiv(col, LD)
    diff = r - col
    decay = jnp.where(same & (diff >= 0), jnp.exp(lg * jnp.maximum(diff, 0).astype(F32)), 0.0)
    o, qb = _ret_core(q, k, v, None, decay, None)
    qdec = jnp.exp(lg * (rt + 1).astype(F32))
    kd = k * jnp.exp(lg * (LD - 1 - rt).astype(F32))
    vb = v.astype(BF16)
    sdec = jnp.exp(lg * LD)
    for s in range(NB):
        mine = rb == s
        s_prev = s_ref[0, s, 0]
        o = o + jnp.where(mine, _dot(qb, s_prev.astype(BF16)) * qdec, 0.0)
        kds = jnp.where(mine, kd, 0.0)
        snew_ref[s, 0] = sdec * s_prev + _dot(kds.T.astype(BF16), vb)
    o_ref[...] = _ret_epilogue(o, gn_ref[...], gr_ref[...]).astype(o_ref.dtype)


def _retention(zA, lg, gn, tabs_p, tabs_s, state, layer, dm):
    H, DK, DV, B, S, Bd, LD, Tp = dm["H"], dm["DK"], dm["DV"], dm["B"], dm["S"], dm["Bd"], dm["LD"], dm["Tp"]
    C = _tile(S, 256)
    nc = S // C
    gn2 = gn.reshape(1, H * DV)
    o_p, s_p = pl.pallas_call(
        functools.partial(_ret_prompt_kernel, C=C, DK=DK),
        grid_spec=pltpu.PrefetchScalarGridSpec(
            num_scalar_prefetch=1,
            grid=(B, H, nc),
            in_specs=[
                pl.BlockSpec((C, DK), lambda b, h, c, lg: (b * nc + c, h)),
                pl.BlockSpec((C, DK), lambda b, h, c, lg: (b * nc + c, H + h)),
                pl.BlockSpec((C, DV), lambda b, h, c, lg: (b * nc + c, 2 * H + h)),
                pl.BlockSpec((C, DV), lambda b, h, c, lg: (b * nc + c, 3 * H + h)),
                pl.BlockSpec((C, DK), lambda b, h, c, lg: (c, 0)),
                pl.BlockSpec((C, DK), lambda b, h, c, lg: (c, 0)),
                pl.BlockSpec((1, DV), lambda b, h, c, lg: (0, h)),
            ],
            out_specs=[
                pl.BlockSpec((C, DV), lambda b, h, c, lg: (b * nc + c, h)),
                pl.BlockSpec((1, 1, DK, DV), lambda b, h, c, lg: (b, h, 0, 0)),
            ],
            scratch_shapes=[pltpu.VMEM((DK, DV), F32)],
        ),
        out_shape=[jax.ShapeDtypeStruct((Tp, H * DV), BF16),
                   jax.ShapeDtypeStruct((B, H, DK, DV), F32)],
        compiler_params=_cp("parallel", "parallel", "arbitrary"),
        name="retention_prompt",
    )(lg, zA, zA, zA, zA, tabs_p[0], tabs_p[1], gn2)

    NB = _tile(Bd, 8, 1)
    L = NB * LD
    off = Tp // L
    o_s, s_s = pl.pallas_call(
        functools.partial(_ret_sample_kernel, NB=NB, LD=LD, DK=DK),
        grid_spec=pltpu.PrefetchScalarGridSpec(
            num_scalar_prefetch=1,
            grid=(Bd // NB, H),
            in_specs=[
                pl.BlockSpec((L, DK), lambda i, h, lg: (off + i, h)),
                pl.BlockSpec((L, DK), lambda i, h, lg: (off + i, H + h)),
                pl.BlockSpec((L, DV), lambda i, h, lg: (off + i, 2 * H + h)),
                pl.BlockSpec((L, DV), lambda i, h, lg: (off + i, 3 * H + h)),
                pl.BlockSpec((L, DK), lambda i, h, lg: (0, 0)),
                pl.BlockSpec((L, DK), lambda i, h, lg: (0, 0)),
                pl.BlockSpec((1, DV), lambda i, h, lg: (0, h)),
                pl.BlockSpec((1, NB, 1, DK, DV), lambda i, h, lg: (layer, i, h, 0, 0)),
            ],
            out_specs=[
                pl.BlockSpec((L, DV), lambda i, h, lg: (i, h)),
                pl.BlockSpec((NB, 1, DK, DV), lambda i, h, lg: (i, h, 0, 0)),
            ],
        ),
        out_shape=[jax.ShapeDtypeStruct((Bd * LD, H * DV), BF16),
                   jax.ShapeDtypeStruct((Bd, H, DK, DV), F32)],
        compiler_params=_cp("parallel", "parallel"),
        name="retention_sample",
    )(lg, zA, zA, zA, zA, tabs_s[0], tabs_s[1], gn2, state)
    return jnp.concatenate([o_p, o_s], axis=0), s_p, s_s


def _pool_prompt_kernel(u_ref, halo_ref, w_ref, sc_ref, o_ref, full_ref, *, TL, HALO, GC):
    i = pl.program_id(1)
    u = u_ref[...]
    full_ref[pl.ds(HALO, TL), :] = u
    full_ref[pl.ds(0, HALO), :] = jnp.where(i == 0, 0.0, halo_ref[...])
    pos = i * TL + lax.broadcasted_iota(I32, (TL, 1), 0)
    for g, w in enumerate(POOL_WINDOWS):
        cs = slice(g * GC, (g + 1) * GC)
        win = full_ref[pl.ds(HALO, TL), cs]
        for j in range(1, w):
            win = win + full_ref[pl.ds(HALO - j, TL), cs]
        cnt = jnp.minimum(w, pos + 1).astype(F32)
        pooled = win / cnt - u[:, cs]
        z = _dot(pooled.astype(BF16), w_ref[g])
        o_ref[:, cs] = (z * sc_ref[:, cs]).astype(o_ref.dtype)


def _pool_sample_kernel(u_ref, buf_ref, w_ref, sc_ref, o_ref, nbuf_ref, *, LD, PB, GC, pos0):
    rows = [buf_ref[0, j] for j in range(PB)] + [u_ref[:, t, :] for t in range(LD)]
    for j in range(PB):
        nbuf_ref[j] = rows[LD + j]
    for g, w in enumerate(POOL_WINDOWS):
        cs = slice(g * GC, (g + 1) * GC)
        pooled = []
        for t in range(LD):
            win = rows[PB + t][:, cs]
            for j in range(1, w):
                win = win + rows[PB + t - j][:, cs]
            cnt = float(min(w, pos0 + t + 1))
            pooled.append(win / cnt - rows[PB + t][:, cs])
        z = _dot(jnp.concatenate(pooled, axis=0).astype(BF16), w_ref[g])
        nb = u_ref.shape[0]
        for t in range(LD):
            o_ref[:, t, cs] = (z[t * nb:(t + 1) * nb] * sc_ref[:, cs]).astype(o_ref.dtype)


def _pooling(zA, pool_w, pool_scale, buf, layer, dm):
    B, S, Bd, LD, Tp, PW, PB = dm["B"], dm["S"], dm["Bd"], dm["LD"], dm["Tp"], dm["PW"], dm["PB"]
    G = len(POOL_WINDOWS)
    GC = PW // G
    HALO = 16
    assert PB < HALO and S % HALO == 0 and max(POOL_WINDOWS) <= HALO
    TL = _tile(S, 512, HALO)
    nt = S // TL
    ucol = dm["u_off"] // PW
    wb = pool_w.astype(BF16)
    sc = pool_scale.reshape(1, PW)
    o_p = pl.pallas_call(
        functools.partial(_pool_prompt_kernel, TL=TL, HALO=HALO, GC=GC),
        grid=(B, nt),
        in_specs=[
            pl.BlockSpec((TL, PW), lambda b, i: (b * nt + i, ucol)),
            pl.BlockSpec((HALO, PW), lambda b, i: (jnp.maximum((b * nt + i) * (TL // HALO) - 1, 0), ucol)),
            pl.BlockSpec((G, GC, GC), lambda b, i: (0, 0, 0)),
            pl.BlockSpec((1, PW), lambda b, i: (0, 0)),
        ],
        out_specs=pl.BlockSpec((TL, PW), lambda b, i: (b * nt + i, 0)),
        out_shape=jax.ShapeDtypeStruct((Tp, PW), BF16),
        scratch_shapes=[pltpu.VMEM((TL + HALO, PW), F32)],
        compiler_params=_cp("parallel", "parallel"),
        name="pool_prompt",
    )(zA, zA, wb, sc)

    u_s = zA[Tp:, dm["u_off"]:dm["u_off"] + PW].reshape(Bd, LD, PW)
    NB = _tile(Bd, 32, 8)
    o_s, nbuf = pl.pallas_call(
        functools.partial(_pool_sample_kernel, LD=LD, PB=PB, GC=GC, pos0=dm["P0"]),
        grid=(Bd // NB,),
        in_specs=[
            pl.BlockSpec((NB, LD, PW), lambda i: (i, 0, 0)),
            pl.BlockSpec((1, PB, NB, PW), lambda i: (layer, 0, i, 0)),
            pl.BlockSpec((G, GC, GC), lambda i: (0, 0, 0)),
            pl.BlockSpec((1, PW), lambda i: (0, 0)),
        ],
        out_specs=[pl.BlockSpec((NB, LD, PW), lambda i: (i, 0, 0)),
                   pl.BlockSpec((PB, NB, PW), lambda i: (0, i, 0))],
        out_shape=[jax.ShapeDtypeStruct((Bd, LD, PW), BF16), jax.ShapeDtypeStruct((PB, Bd, PW), F32)],
        compiler_params=_cp("parallel"),
        name="pool_sample",
    )(u_s, buf, wb, sc)
    pool_p = zA[:Tp, dm["u_off"]:dm["u_off"] + PW].reshape(B, S, PW)[:, S - PB:]
    return jnp.concatenate([o_p, o_s.reshape(Bd * LD, PW)], axis=0), pool_p, nbuf


def _qprep_kernel(cq_ref, g_ref, w_ref, cos_ref, sa_ref, sb_ref, gn_ref, gr_ref, q_ref,
                  *, MH, NOPE, ROPE, scale):
    x = cq_ref[...]
    y = x * lax.rsqrt(jnp.mean(x * x, axis=-1, keepdims=True) + EPS) * g_ref[...]
    qq = _dot(y.astype(BF16), w_ref[...])
    qn = qq[:, :MH * NOPE]
    qr = qq[:, MH * NOPE:]
    W = MH * ROPE
    qr = (qr * cos_ref[...] + pltpu.roll(qr, W - ROPE // 2, 1) * sa_ref[...]
          + pltpu.roll(qr, ROPE // 2, 1) * sb_ref[...])
    gn = gn_ref[...]
    gr = gr_ref[...]
    for h in range(MH):
        a = qn[:, h * NOPE:(h + 1) * NOPE]
        b = qr[:, h * ROPE:(h + 1) * ROPE]
        ss = jnp.sum(a * a, axis=-1, keepdims=True) + jnp.sum(b * b, axis=-1, keepdims=True)
        inv = lax.rsqrt(ss * (1.0 / (NOPE + ROPE)) + EPS) * scale
        q_ref[h, :, :NOPE] = (a * inv * gn).astype(q_ref.dtype)
        q_ref[h, :, NOPE:] = (b * inv * gr).astype(q_ref.dtype)


def _kvprep_kernel(ckv_ref, kpe_ref, g_ref, wuk_ref, wuv_ref, cos_ref, sa_ref, sb_ref,
                   rows_ref, k_ref, v_ref, *, MH, NOPE, ROPE, KV):
    x = ckv_ref[...]
    lat = x * lax.rsqrt(jnp.mean(x * x, axis=-1, keepdims=True) + EPS) * g_ref[...]
    kp = kpe_ref[...]
    W = kp.shape[1]
    kp = (kp * cos_ref[...] + pltpu.roll(kp, W - ROPE // 2, 1) * sa_ref[...]
          + pltpu.roll(kp, ROPE // 2, 1) * sb_ref[...])
    rows_ref[:, :KV] = lat
    rows_ref[:, KV:] = kp[:, :ROPE]
    latb = lat.astype(BF16)
    kn = _dot(latb, wuk_ref[...])
    v_ref[...] = _dot(latb, wuv_ref[...]).astype(v_ref.dtype)
    sspe = jnp.sum(kp * kp, axis=-1, keepdims=True)
    for h in range(MH):
        a = kn[:, h * NOPE:(h + 1) * NOPE]
        ss = jnp.sum(a * a, axis=-1, keepdims=True) + sspe
        inv = lax.rsqrt(ss * (1.0 / (NOPE + ROPE)) + EPS)
        k_ref[h, :, :NOPE] = (a * inv).astype(k_ref.dtype)
        k_ref[h, :, NOPE:] = (kp[:, :ROPE] * inv).astype(k_ref.dtype)


def _attn_prompt_kernel(q_ref, k_ref, v_ref, o_ref, *, TQ):
    i = pl.program_id(2)
    q = q_ref[0]
    qpos = i * TQ + lax.broadcasted_iota(I32, (TQ, 1), 0)

    def body(j, carry):
        m, l, acc = carry
        start = pl.multiple_of(j * TQ, TQ)
        s = _dot_nt(q, k_ref[0, pl.ds(start, TQ), :])
        kpos = start + lax.broadcasted_iota(I32, (1, TQ), 1)
        s = jnp.where(kpos <= qpos, s, NEG_INF)
        m_new = jnp.maximum(m, jnp.max(s, axis=-1, keepdims=True))
        corr = jnp.exp(m - m_new)
        p = jnp.exp(s - m_new)
        l = l * corr + jnp.sum(p, axis=-1, keepdims=True)
        acc = acc * corr + _dot(p.astype(BF16), v_ref[pl.ds(start, TQ), :])
        return m_new, l, acc

    init = (jnp.full((TQ, 1), NEG_INF, F32), jnp.zeros((TQ, 1), F32), jnp.zeros(o_ref.shape, F32))
    _, l, acc = lax.fori_loop(0, i + 1, body, init)
    o_ref[...] = (acc / l).astype(o_ref.dtype)


def _attn_sample_kernel(pt_ref, qn_ref, qr_ref, self_ref, *rest, PP, MH, NOPE, ROPE, KV, LD, NP_STEPS):
    pages = rest[:PP]
    wukt_ref, wuv_ref, o_ref, qc_ref, m_ref, l_ref, acc_ref = rest[PP:]
    step = pl.program_id(1)
    NQ = LD * MH

    def absorb(chunks, mask):
        s_parts, lat_parts = [], []
        for page_t in chunks:
            W = page_t.shape[1]
            pb = page_t.astype(BF16)
            latb = pb[:KV]
            both = _dot(qc_ref[...], latb)
            kn = both[:MH * NOPE]
            kpe = page_t[KV:]
            ss = jnp.sum((kn * kn).reshape(MH, NOPE, W), axis=1) + jnp.sum(kpe * kpe, axis=0, keepdims=True)
            rinv = lax.rsqrt(ss * (1.0 / (NOPE + ROPE)) + EPS)
            s = both[MH * NOPE:] + _dot(qr_ref[0], pb[KV:])
            s_parts.append(s * jnp.concatenate([rinv] * LD, axis=0))
            lat_parts.append(latb)
        s = s_parts[0] if len(s_parts) == 1 else jnp.concatenate(s_parts, axis=1)
        if mask is not None:
            s = jnp.where(mask, s, NEG_INF)
        m_old = m_ref[...]
        m_new = jnp.maximum(m_old, jnp.max(s, axis=-1, keepdims=True))
        corr = jnp.exp(m_old - m_new)
        p = jnp.exp(s - m_new)
        l_ref[...] = l_ref[...] * corr + jnp.sum(p, axis=-1, keepdims=True)
        pb16 = p.astype(BF16)
        pv, off = None, 0
        for latb in lat_parts:
            W = latb.shape[1]
            part = _dot_nt(pb16[:, off:off + W], latb)
            pv = part if pv is None else pv + part
            off += W
        acc_ref[...] = acc_ref[...] * corr + pv
        m_ref[...] = m_new

    @pl.when(step == 0)
    def _():
        m_ref[...] = jnp.full_like(m_ref, NEG_INF)
        l_ref[...] = jnp.zeros_like(l_ref)
        acc_ref[...] = jnp.zeros_like(acc_ref)
        qc_ref[pl.ds(0, MH * NOPE), :] = wukt_ref[...]
        qc_ref[pl.ds(MH * NOPE, NQ), :] = _dot(qn_ref[0], wukt_ref[...]).astype(BF16)
        page_t = self_ref[0]
        t = _idiv(lax.broadcasted_iota(I32, (NQ, 1), 0), MH)
        j = lax.broadcasted_iota(I32, (1, page_t.shape[1]), 1)
        absorb([page_t], j <= t)

    if PP % 2 == 0:
        absorb([jnp.concatenate([pages[c][0, 0], pages[c + 1][0, 0]], axis=1) for c in range(0, PP, 2)], None)
    else:
        absorb([p[0, 0] for p in pages], None)

    @pl.when(step == NP_STEPS - 1)
    def _():
        o_lat = (acc_ref[...] / l_ref[...]).astype(BF16)
        VD = wuv_ref.shape[1] // MH
        hrow = lax.broadcasted_iota(I32, (MH, MH * VD), 0)
        hcol = _idiv(lax.broadcasted_iota(I32, (MH, MH * VD), 1), VD)
        for t in range(LD):
            full = _dot(o_lat[t * MH:(t + 1) * MH], wuv_ref[...])
            o_ref[0, pl.ds(t, 1), :] = jnp.sum(jnp.where(hrow == hcol, full, 0.0), axis=0,
                                              keepdims=True).astype(o_ref.dtype)


def _mla(zA, lw, tabs_q, tabs_k, cache_t, layer, page_table, dm):
    T, Tp, B, S, Bd, LD = dm["T"], dm["Tp"], dm["B"], dm["S"], dm["Bd"], dm["LD"]
    MH, NOPE, ROPE, KV, QL, VD = dm["MH"], dm["NOPE"], dm["ROPE"], dm["KV"], dm["QL"], dm["VD"]
    QK = NOPE + ROPE
    ROW = KV + ROPE
    scale = QK ** -0.5
    tm = _tile(T, 512)

    w_uq = lw["mla_w_uq"].reshape(QL, MH, QK)
    w_uq = jnp.concatenate([w_uq[:, :, :NOPE].reshape(QL, MH * NOPE),
                            w_uq[:, :, NOPE:].reshape(QL, MH * ROPE)], axis=1).astype(BF16)
    w_ukv = lw["mla_w_ukv"].reshape(KV, MH, NOPE + VD)
    w_uk = w_ukv[:, :, :NOPE].reshape(KV, MH * NOPE).astype(BF16)
    w_uv = w_ukv[:, :, NOPE:].reshape(KV, MH * VD).astype(BF16)
    gq, gk = lw["mla_qn_g"], lw["mla_kn_g"]
    gn = (gq[:NOPE] * gk[:NOPE]).reshape(1, NOPE)
    gr = jnp.tile(gq[NOPE:] * gk[NOPE:], 2).reshape(1, ROPE)

    c_q = zA[:, dm["cq_off"]:dm["cq_off"] + QL]
    q = pl.pallas_call(
        functools.partial(_qprep_kernel, MH=MH, NOPE=NOPE, ROPE=ROPE, scale=scale),
        grid=(T // tm,),
        in_specs=[
            pl.BlockSpec((tm, QL), lambda i: (i, 0)),
            pl.BlockSpec((1, QL), lambda i: (0, 0)),
            pl.BlockSpec((QL, MH * QK), lambda i: (0, 0)),
            pl.BlockSpec((tm, MH * ROPE), lambda i: (i, 0)),
            pl.BlockSpec((tm, MH * ROPE), lambda i: (i, 0)),
            pl.BlockSpec((tm, MH * ROPE), lambda i: (i, 0)),
            pl.BlockSpec((1, NOPE), lambda i: (0, 0)),
            pl.BlockSpec((1, ROPE), lambda i: (0, 0)),
        ],
        out_specs=pl.BlockSpec((MH, tm, QK), lambda i: (0, i, 0)),
        out_shape=jax.ShapeDtypeStruct((MH, T, QK), BF16),
        compiler_params=_cp("parallel"),
        name="mla_q_prep",
    )(c_q, lw["mla_qnorm_g"].reshape(1, QL), w_uq, tabs_q[0], tabs_q[1], tabs_q[2], gn, gr)

    kvb = dm["kv_off"] // KV
    kpb = dm["kpe_off"] // (2 * ROPE)
    rows, k, v = pl.pallas_call(
        functools.partial(_kvprep_kernel, MH=MH, NOPE=NOPE, ROPE=ROPE, KV=KV),
        grid=(T // tm,),
        in_specs=[
            pl.BlockSpec((tm, KV), lambda i: (i, kvb)),
            pl.BlockSpec((tm, 2 * ROPE), lambda i: (i, kpb)),
            pl.BlockSpec((1, KV), lambda i: (0, 0)),
            pl.BlockSpec((KV, MH * NOPE), lambda i: (0, 0)),
            pl.BlockSpec((KV, MH * VD), lambda i: (0, 0)),
            pl.BlockSpec((tm, 2 * ROPE), lambda i: (i, 0)),
            pl.BlockSpec((tm, 2 * ROPE), lambda i: (i, 0)),
            pl.BlockSpec((tm, 2 * ROPE), lambda i: (i, 0)),
        ],
        out_specs=[
            pl.BlockSpec((tm, ROW), lambda i: (i, 0)),
            pl.BlockSpec((MH, tm, QK), lambda i: (0, i, 0)),
            pl.BlockSpec((tm, MH * VD), lambda i: (i, 0)),
        ],
        out_shape=[jax.ShapeDtypeStruct((T, ROW), F32),
                   jax.ShapeDtypeStruct((MH, T, QK), BF16),
                   jax.ShapeDtypeStruct((T, MH * VD), BF16)],
        compiler_params=_cp("parallel"),
        name="mla_kv_prep",
    )(zA, zA, lw["mla_kvnorm_g"].reshape(1, KV), w_uk, w_uv, tabs_k[0], tabs_k[1], tabs_k[2])

    TQ = _tile(S, 512)
    nq = S // TQ
    o_p = pl.pallas_call(
        functools.partial(_attn_prompt_kernel, TQ=TQ),
        grid=(B, MH, nq),
        in_specs=[
            pl.BlockSpec((1, TQ, QK), lambda b, h, i: (h, b * nq + i, 0)),
            pl.BlockSpec((1, S, QK), lambda b, h, i: (h, b, 0)),
            pl.BlockSpec((S, VD), lambda b, h, i: (b, h)),
        ],
        out_specs=pl.BlockSpec((TQ, VD), lambda b, h, i: (b * nq + i, h)),
        out_shape=jax.ShapeDtypeStruct((Tp, MH * VD), BF16),
        compiler_params=_cp("parallel", "parallel", "parallel"),
        name="mla_attn_prompt",
    )(q, k, v)

    PAGE = cache_t.shape[3]
    NP = page_table.shape[1]
    PP = _tile(NP, 8, 1)
    NQ = LD * MH
    qs = q[:, Tp:].reshape(MH, Bd, LD, QK).transpose(1, 2, 0, 3).reshape(Bd, NQ, QK)
    head_of_row = jnp.arange(NQ) % MH
    eye = (head_of_row[:, None] == jnp.arange(MH)[None, :])
    qn_bd = jnp.where(eye[None, :, :, None], qs[:, :, None, :NOPE], 0).reshape(Bd, NQ, MH * NOPE)
    qr_s = qs[:, :, NOPE:]
    self_t = jnp.pad(rows[Tp:].reshape(Bd, LD, ROW).transpose(0, 2, 1), ((0, 0), (0, 0), (0, PAGE - LD)))
    page_specs = [
        pl.BlockSpec((1, 1, ROW, PAGE), functools.partial(
            lambda b, s, pt, i: (layer, pt[b, s * PP + i], 0, 0), i=i))
        for i in range(PP)
    ]
    o_s = pl.pallas_call(
        functools.partial(_attn_sample_kernel, PP=PP, MH=MH, NOPE=NOPE, ROPE=ROPE, KV=KV, LD=LD,
                          NP_STEPS=NP // PP),
        grid_spec=pltpu.PrefetchScalarGridSpec(
            num_scalar_prefetch=1,
            grid=(Bd, NP // PP),
            in_specs=[
                pl.BlockSpec((1, NQ, MH * NOPE), lambda b, s, pt: (b, 0, 0)),
                pl.BlockSpec((1, NQ, ROPE), lambda b, s, pt: (b, 0, 0)),
                pl.BlockSpec((1, ROW, PAGE), lambda b, s, pt: (b, 0, 0)),
                *page_specs,
                pl.BlockSpec((MH * NOPE, KV), lambda b, s, pt: (0, 0)),
                pl.BlockSpec((KV, MH * VD), lambda b, s, pt: (0, 0)),
            ],
            out_specs=pl.BlockSpec((1, LD, MH * VD), lambda b, s, pt: (b, 0, 0)),
            scratch_shapes=[pltpu.VMEM((MH * NOPE + NQ, KV), BF16), pltpu.VMEM((NQ, 1), F32), pltpu.VMEM((NQ, 1), F32),
                            pltpu.VMEM((NQ, KV), F32)],
        ),
        out_shape=jax.ShapeDtypeStruct((Bd, LD, MH * VD), BF16),
        compiler_params=_cp("parallel", "arbitrary"),
        name="mla_attn_sample",
    )(page_table, qn_bd, qr_s, self_t, *([cache_t] * PP), w_uk.T, w_uv)
    o_m = jnp.concatenate([o_p, o_s.reshape(Bd * LD, MH * VD)], axis=0)
    return o_m, rows


def _merge_kernel(or_ref, op_ref, om_ref, wr_ref, wp_ref, wm_ref, g0_ref, g1_ref, g2_ref, o_ref):
    acc = jax.nn.sigmoid(g0_ref[...]) * _dot(or_ref[...], wr_ref[0])
    acc = acc + jax.nn.sigmoid(g1_ref[...]) * _dot(op_ref[...], wp_ref[0])
    acc = acc + jax.nn.sigmoid(g2_ref[...]) * _dot(om_ref[...], wm_ref[0])
    o_ref[...] = acc.astype(o_ref.dtype)


def _merge(o_r, o_p, o_m, w_r, w_p, w_m, layer, gl, dm):
    T, D = dm["T"], dm["D"]
    tm = _tile(T, 512)
    tn = _tile(D, 1024, LANES)
    nj = D // tn
    branch = lambda w: pl.BlockSpec((tm, w.shape[1]), lambda j, i: (i, 0))
    weight = lambda w: pl.BlockSpec((1, w.shape[1], tn), lambda j, i: (layer, 0, j))
    gate = lambda n: pl.BlockSpec((tm, tn), lambda j, i: (i, n * nj + j))
    return pl.pallas_call(
        _merge_kernel,
        grid=(nj, T // tm),
        in_specs=[branch(w_r), branch(w_p), branch(w_m), weight(w_r), weight(w_p), weight(w_m),
                  gate(0), gate(1), gate(2)],
        out_specs=pl.BlockSpec((tm, tn), lambda j, i: (i, j)),
        out_shape=jax.ShapeDtypeStruct((T, D), BF16),
        compiler_params=_cp("parallel", "parallel"),
        name="gated_merge",
    )(o_r, o_p, o_m, w_r, w_p, w_m, gl, gl, gl)


def _router_kernel(x_ref, g_ref, w_ref, b_ref, h_ref, r_ref):
    x = x_ref[...]
    h = x * lax.rsqrt(jnp.mean(x * x, axis=-1, keepdims=True) + EPS) * g_ref[...]
    half = x.shape[1] // 2
    bits = lax.bitcast_convert_type(h.astype(BF16).astype(F32), jnp.uint32)
    h_ref[...] = (bits[:, :half] >> 16) | (bits[:, half:] & jnp.uint32(0xFFFF0000))
    logits = jnp.dot(h, w_ref[...], preferred_element_type=F32, precision=lax.Precision.HIGHEST) + b_ref[...]
    lane = lax.broadcasted_iota(I32, logits.shape, 1)
    big = jnp.int32(1 << 20)
    gl = jnp.where(lane < N_GROUPS, logits, -jnp.inf)
    gmax = jnp.max(gl, axis=-1, keepdims=True)
    g_idx = jnp.min(jnp.where(gl == gmax, lane, big), axis=-1, keepdims=True)
    g_w = 1.0 / jnp.sum(jnp.exp(gl - gmax), axis=-1, keepdims=True)
    lo = N_GROUPS + g_idx * EXPERTS_PER_GROUP
    sel = (lane >= lo) & (lane < lo + EXPERTS_PER_GROUP)
    el = jnp.where(sel, logits, -jnp.inf)
    pe = jnp.exp(el - jnp.max(el, axis=-1, keepdims=True))
    p = pe / jnp.sum(pe, axis=-1, keepdims=True)
    p = jnp.where(sel, p, -1.0)
    p1 = jnp.max(p, axis=-1, keepdims=True)
    i1 = jnp.min(jnp.where(p == p1, lane, big), axis=-1, keepdims=True)
    prest = jnp.where(lane == i1, -1.0, p)
    p2 = jnp.max(prest, axis=-1, keepdims=True)
    i2 = jnp.min(jnp.where(prest == p2, lane, big), axis=-1, keepdims=True)
    den = p1 + p2
    w1 = g_w * p1 / den
    w2 = g_w * p2 / den
    out = jnp.where(lane == 0, (i1 - N_GROUPS).astype(F32),
                    jnp.where(lane == 1, (i2 - N_GROUPS).astype(F32),
                              jnp.where(lane == 2, w1, jnp.where(lane == 3, w2, 0.0))))
    r_ref[...] = out


def _moe_up_kernel(te_ref, tok_ref, nu_ref, h_hbm, wg_ref, wu_ref, a_ref, xbuf, sem, *, TM):
    i = pl.program_id(0)
    n_used = nu_ref[0]
    slot = i % 2

    def row_copy(s, r, tok):
        return pltpu.make_async_copy(h_hbm.at[pl.ds(tok, 1)], xbuf.at[s, pl.ds(r, 1)], sem.at[s])

    def gather(tile, s):
        def issue(r, c):
            row_copy(s, r, tok_ref[tile * TM + r]).start()
            return c

        lax.fori_loop(0, TM, issue, 0)

    @pl.when((i == 0) & (n_used > 0))
    def _():
        gather(0, 0)

    @pl.when(i + 1 < n_used)
    def _():
        gather(i + 1, 1 - slot)

    @pl.when(i < n_used)
    def _():
        def drain(r, c):
            row_copy(slot, r, 0).wait()
            return c

        lax.fori_loop(0, TM, drain, 0)
        w = xbuf[slot]
        x = jnp.concatenate([lax.bitcast_convert_type(w << 16, F32),
                             lax.bitcast_convert_type(w & jnp.uint32(0xFFFF0000), F32)], axis=1).astype(BF16)
        g = _dot(x, wg_ref[0, 0])
        u = _dot(x, wu_ref[0, 0])
        a_ref[...] = (g * jax.nn.sigmoid(g) * u).astype(a_ref.dtype)

    @pl.when(i >= n_used)
    def _():
        a_ref[...] = jnp.zeros_like(a_ref)


def _moe_down_kernel(te_ref, nu_ref, a_ref, wd_ref, rw_ref, y_ref):
    i = pl.program_id(0)

    @pl.when(i < nu_ref[0])
    def _():
        y_ref[...] = _dot(a_ref[...], wd_ref[0, 0]) * rw_ref[...]

    @pl.when(i >= nu_ref[0])
    def _():
        y_ref[...] = jnp.zeros_like(y_ref)


def _moe_combine_kernel(p1_ref, p2_ref, x_ref, y_hbm, o_ref, b1, b2, sem, *, TM, NT):
    i = pl.program_id(0)
    slot = i % 2

    def row_copy(s, src_row, dst, r):
        return pltpu.make_async_copy(y_hbm.at[pl.ds(src_row, 1)], dst.at[s, pl.ds(r, 1)], sem.at[s])

    def gather(tile, s):
        def issue(r, c):
            row_copy(s, p1_ref[tile * TM + r], b1, r).start()
            row_copy(s, p2_ref[tile * TM + r], b2, r).start()
            return c

        lax.fori_loop(0, TM, issue, 0)

    @pl.when(i == 0)
    def _():
        gather(0, 0)

    @pl.when(i + 1 < NT)
    def _():
        gather(i + 1, 1 - slot)

    def drain(r, c):
        row_copy(slot, 0, b1, r).wait()
        row_copy(slot, 0, b2, r).wait()
        return c

    lax.fori_loop(0, TM, drain, 0)
    o_ref[...] = x_ref[...] + (b1[slot] + b2[slot])


def _moe(x1, norm_g, wr, br, wg, wu, wd, layer, dm):
    T, D = dm["T"], dm["D"]
    E, DE = wg.shape[1], wg.shape[3]
    tm = _tile(T, 256)
    h2, route = pl.pallas_call(
        _router_kernel,
        grid=(T // tm,),
        in_specs=[
            pl.BlockSpec((tm, D), lambda i: (i, 0)),
            pl.BlockSpec((1, D), lambda i: (0, 0)),
            pl.BlockSpec((D, ROUTE_LANES), lambda i: (0, 0)),
            pl.BlockSpec((1, ROUTE_LANES), lambda i: (0, 0)),
        ],
        out_specs=[pl.BlockSpec((tm, D // 2), lambda i: (i, 0)), pl.BlockSpec((tm, ROUTE_LANES), lambda i: (i, 0))],
        out_shape=[jax.ShapeDtypeStruct((T, D // 2), jnp.uint32), jax.ShapeDtypeStruct((T, ROUTE_LANES), F32)],
        compiler_params=_cp("parallel"),
        name="moe_router",
    )(x1, norm_g.reshape(1, D), wr, br)

    TM = MOE_TM
    n_tiles = pl.cdiv(2 * T, TM) + E
    P = n_tiles * TM
    ids = route[:, :2].astype(I32).reshape(-1)
    wts = route[:, 2:4].reshape(-1)
    onehot = (ids[:, None] == jnp.arange(E, dtype=I32)[None, :]).astype(I32)
    csum = jnp.cumsum(onehot, axis=0)
    rank = jnp.sum(jnp.where(onehot > 0, csum, 0), axis=1) - 1
    counts = csum[-1]
    padded = ((counts + TM - 1) // TM) * TM
    ends = jnp.cumsum(padded)
    starts = ends - padded
    pos = (starts[ids] + rank).astype(I32)
    tok_and_w = jnp.stack([(jnp.arange(2 * T, dtype=I32) // 2).astype(F32), wts], axis=1)
    row_meta = jnp.zeros((P, 2), F32).at[pos].set(tok_and_w)
    row_tok = row_meta[:, 0].astype(I32)
    row_w = row_meta[:, 1:2]
    tile_start = jnp.arange(n_tiles, dtype=I32) * TM
    tile_e = jnp.minimum(jnp.sum((ends[None, :] <= tile_start[:, None]).astype(I32), axis=1), E - 1)
    n_used = (ends[-1] // TM).astype(I32).reshape(1)
    pos2 = pos.reshape(T, 2)

    act = pl.pallas_call(
        functools.partial(_moe_up_kernel, TM=TM),
        grid_spec=pltpu.PrefetchScalarGridSpec(
            num_scalar_prefetch=3,
            grid=(n_tiles,),
            in_specs=[
                pl.BlockSpec(memory_space=pl.ANY),
                pl.BlockSpec((1, 1, D, DE), lambda i, te, tok, nu: (layer, te[i], 0, 0)),
                pl.BlockSpec((1, 1, D, DE), lambda i, te, tok, nu: (layer, te[i], 0, 0)),
            ],
            out_specs=pl.BlockSpec((TM, DE), lambda i, te, tok, nu: (i, 0)),
            scratch_shapes=[pltpu.VMEM((2, TM, D // 2), jnp.uint32), pltpu.SemaphoreType.DMA((2,))],
        ),
        out_shape=jax.ShapeDtypeStruct((P, DE), BF16),
        compiler_params=_cp("arbitrary"),
        name="moe_gather_up",
    )(tile_e, row_tok, n_used, h2, wg, wu)

    ys = pl.pallas_call(
        _moe_down_kernel,
        grid_spec=pltpu.PrefetchScalarGridSpec(
            num_scalar_prefetch=2,
            grid=(n_tiles,),
            in_specs=[
                pl.BlockSpec((TM, DE), lambda i, te, nu: (i, 0)),
                pl.BlockSpec((1, 1, DE, D), lambda i, te, nu: (layer, te[i], 0, 0)),
                pl.BlockSpec((TM, 1), lambda i, te, nu: (i, 0)),
            ],
            out_specs=pl.BlockSpec((TM, D), lambda i, te, nu: (i, 0)),
        ),
        out_shape=jax.ShapeDtypeStruct((P, D), F32),
        compiler_params=_cp("arbitrary"),
        name="moe_down",
    )(tile_e, n_used, act, wd, row_w)

    tc = _tile(T, 256)
    return pl.pallas_call(
        functools.partial(_moe_combine_kernel, TM=tc, NT=T // tc),
        grid_spec=pltpu.PrefetchScalarGridSpec(
            num_scalar_prefetch=2,
            grid=(T // tc,),
            in_specs=[
                pl.BlockSpec((tc, D), lambda i, p1, p2: (i, 0)),
                pl.BlockSpec(memory_space=pl.ANY),
            ],
            out_specs=pl.BlockSpec((tc, D), lambda i, p1, p2: (i, 0)),
            scratch_shapes=[pltpu.VMEM((2, tc, D), F32), pltpu.VMEM((2, tc, D), F32),
                            pltpu.SemaphoreType.DMA((2,))],
        ),
        out_shape=jax.ShapeDtypeStruct((T, D), F32),
        compiler_params=_cp("arbitrary"),
        name="moe_combine",
    )(pos2[:, 0], pos2[:, 1], x1, ys)


def _rope_cos_sin(pos, half):
    inv = ROPE_BASE ** (-jnp.arange(half, dtype=F32) / half)
    ang = pos.astype(F32)[:, None] * inv[None, :]
    return jnp.cos(ang), jnp.sin(ang)


def _ret_tables(pos, dk):
    cos, sin = _rope_cos_sin(pos, dk // 2)
    return jnp.concatenate([cos, cos], axis=1), jnp.concatenate([-sin, sin], axis=1)


def _mla_tables(pos, rope, reps, pad):
    cos, sin = _rope_cos_sin(pos, rope // 2)
    z = jnp.zeros_like(sin)
    tabs = (jnp.concatenate([cos, cos], axis=1), jnp.concatenate([-sin, z], axis=1),
            jnp.concatenate([z, sin], axis=1))
    if pad:
        return tuple(jnp.pad(t, ((0, 0), (0, pad))) for t in tabs)
    return tuple(jnp.tile(t, (1, reps)) for t in tabs)


def kernel(x_prompt, x_sample, cache_mla, page_table, state_ret, state_pool, norm1_g, w_in, ret_gn_g, pool_w, pool_scale, mla_qnorm_g, mla_w_uq, mla_kvnorm_g, mla_w_ukv, mla_qn_g, mla_kn_g, w_br_ret, w_br_pool, w_br_mla, w_out, norm2_g, router_g_w, router_g_b, router_e_w, router_e_b, exp_w_gate, exp_w_up, exp_w_down):
    B, S, D = x_prompt.shape
    Bd, LD, _ = x_sample.shape
    DEPTH = w_in.shape[0]
    H, DK, DV = state_ret.shape[2:]
    PB, PW = state_pool.shape[2:]
    QL = mla_qnorm_g.shape[1]
    KV = mla_kvnorm_g.shape[1]
    ROW = cache_mla.shape[3]
    ROPE = ROW - KV
    NOPE = mla_qn_g.shape[1] - ROPE // 2
    MH = mla_w_uq.shape[2] // (NOPE + ROPE)
    VD = mla_w_ukv.shape[2] // MH - NOPE
    P0 = page_table.shape[1] * cache_mla.shape[2]
    E = exp_w_gate.shape[1]
    assert E == N_GROUPS * EXPERTS_PER_GROUP and 2 * ROPE == LANES
    Tp, Ts = B * S, Bd * LD
    T = Tp + Ts
    u_off = 2 * H * DK + 2 * H * DV
    cq_off = u_off + PW
    kv_off = cq_off + QL
    kpe_off = kv_off + KV
    NA = kpe_off + ROPE
    NA_pad = -(-(NA + ROPE) // LANES) * LANES
    assert w_in.shape[2] == NA + 3 * D and kpe_off % LANES == 0 and kv_off % KV == 0 and u_off % PW == 0
    dm = dict(B=B, S=S, D=D, Bd=Bd, LD=LD, H=H, DK=DK, DV=DV, PB=PB, PW=PW, QL=QL, KV=KV, ROPE=ROPE,
              NOPE=NOPE, MH=MH, VD=VD, P0=P0, Tp=Tp, T=T, u_off=u_off, cq_off=cq_off, kv_off=kv_off,
              kpe_off=kpe_off)

    pos_p = jnp.arange(S)
    pos_s = P0 + jnp.arange(LD)
    pos_all = jnp.concatenate([jnp.tile(pos_p, B), jnp.tile(pos_s, Bd)])
    nb_ret = _tile(Bd, 8, 1)
    tabs_ret_p = _ret_tables(pos_p, DK)
    tabs_ret_s = _ret_tables(jnp.tile(pos_s, nb_ret), DK)
    tabs_q = _mla_tables(pos_all, ROPE, MH, 0)
    tabs_k = _mla_tables(pos_all, ROPE, 1, ROPE)
    lg = jnp.log1p(-jnp.exp2(-5.0 - jnp.arange(H, dtype=F32)))

    x = jnp.concatenate([x_prompt.reshape(Tp, D), x_sample.reshape(Ts, D)], axis=0)
    tmA = _tile(T, 512)
    tnA = _tile(NA_pad, 1024, LANES)
    tnD = _tile(D, 1024, LANES)
    rows_p, rows_s, ret_p, ret_s, pool_p, pool_s = [], [], [], [], [], []
    cache_t = jnp.swapaxes(cache_mla, 2, 3)
    w_in_t = jnp.swapaxes(w_in, 1, 2)
    state_pool_t = jnp.swapaxes(state_pool, 1, 2)
    w_in_b = w_in_t.astype(BF16)
    w_gates_b = w_in_b[:, NA:]
    w_br_ret_b, w_br_pool_b, w_br_mla_b = (w.astype(BF16) for w in (w_br_ret, w_br_pool, w_br_mla))
    w_out_b = w_out.astype(BF16)
    exp_gate_b, exp_up_b, exp_down_b = (w.astype(BF16) for w in (exp_w_gate, exp_w_up, exp_w_down))
    for l in range(DEPTH):
        lw = {"mla_qnorm_g": mla_qnorm_g[l], "mla_w_uq": mla_w_uq[l], "mla_kvnorm_g": mla_kvnorm_g[l],
              "mla_w_ukv": mla_w_ukv[l], "mla_qn_g": mla_qn_g[l], "mla_kn_g": mla_kn_g[l]}
        h = _rmsnorm(x, norm1_g[l], BF16)
        zA = _matmul(h, w_in_b, l, tm=tmA, tn=tnA, n=NA_pad, b_is_nk=True, name="in_proj_main")
        gl = _matmul(h, w_gates_b, l, tm=tmA, tn=tnD, b_is_nk=True, name="in_proj_gates")
        o_r, s_p, s_s = _retention(zA, lg, ret_gn_g[l], tabs_ret_p, tabs_ret_s, state_ret, l, dm)
        o_p, b_p, b_s = _pooling(zA, pool_w[l], pool_scale[l], state_pool_t, l, dm)
        o_m, rows = _mla(zA, lw, tabs_q, tabs_k, cache_t, l, page_table, dm)
        merged = _merge(o_r, o_p, o_m, w_br_ret_b, w_br_pool_b, w_br_mla_b, l, gl, dm)
        x1 = _matmul(merged, w_out_b, l, tm=tmA, tn=tnD, res=x, name="out_proj")
        wr = jnp.pad(jnp.concatenate([router_g_w[l], router_e_w[l]], axis=1),
                     ((0, 0), (0, ROUTE_LANES - N_GROUPS - E)))
        br = jnp.pad(jnp.concatenate([router_g_b[l], router_e_b[l]]), (0, ROUTE_LANES - N_GROUPS - E))
        x = _moe(x1, norm2_g[l], wr, br.reshape(1, ROUTE_LANES), exp_gate_b, exp_up_b, exp_down_b, l, dm)
        rows_p.append(rows[:Tp].reshape(B, S, ROW))
        rows_s.append(rows[Tp:].reshape(Bd, LD, ROW))
        ret_p.append(s_p)
        ret_s.append(s_s)
        pool_p.append(b_p)
        pool_s.append(b_s)
    return (x[:Tp].reshape(B, S, D), x[Tp:].reshape(Bd, LD, D), jnp.stack(rows_p), jnp.stack(rows_s),
            jnp.stack(ret_p), jnp.stack(ret_s), jnp.stack(pool_p), jnp.swapaxes(jnp.stack(pool_s), 1, 2))
```

```python
import functools

import jax
import jax.numpy as jnp
from jax import lax
from jax.experimental import pallas as pl
from jax.experimental.pallas import tpu as pltpu

F32 = jnp.float32
BF16 = jnp.bfloat16
I32 = jnp.int32

EPS = 1e-6
ROPE_BASE = 10000.0
NEG_INF = -1e30
POOL_WINDOWS = (2, 4, 8, 16)
N_GROUPS = 4
EXPERTS_PER_GROUP = 4
LANES = 128
VMEM_LIMIT = 56 * 1024 * 1024
ROUTE_LANES = 128
MOE_TM = 256
PAGE_SLOTS = 3


def _cp(*sem):
    return pltpu.CompilerParams(dimension_semantics=sem, vmem_limit_bytes=VMEM_LIMIT)


def _tile(n, pref, mult=8):
    for t in range(min(pref, n), 0, -1):
        if n % t == 0 and t % mult == 0:
            return t
    return n


def _dot(a, b):
    return jnp.dot(a, b, preferred_element_type=F32)


def _dot_nt(a, b):
    return lax.dot_general(a, b, (((1,), (1,)), ((), ())), preferred_element_type=F32)


def _idiv(x, d):
    return jnp.floor((x.astype(F32) + 0.5) * (1.0 / d)).astype(I32)


def _rmsnorm_kernel(x_ref, g_ref, o_ref):
    x = x_ref[...]
    y = x * lax.rsqrt(jnp.mean(x * x, axis=-1, keepdims=True) + EPS)
    o_ref[...] = (y * g_ref[...]).astype(o_ref.dtype)


def _rmsnorm(x, g, out_dtype):
    T, D = x.shape
    tm = _tile(T, 512)
    return pl.pallas_call(
        _rmsnorm_kernel,
        grid=(T // tm,),
        in_specs=[pl.BlockSpec((tm, D), lambda i: (i, 0)), pl.BlockSpec((1, D), lambda i: (0, 0))],
        out_specs=pl.BlockSpec((tm, D), lambda i: (i, 0)),
        out_shape=jax.ShapeDtypeStruct((T, D), out_dtype),
        compiler_params=_cp("parallel"),
        name="rmsnorm",
    )(x, g.reshape(1, D))


def _mm_kernel(a_ref, b_ref, *rest, has_res, b_is_nk):
    if has_res:
        r_ref, o_ref = rest
    else:
        (o_ref,) = rest
    acc = _dot_nt(a_ref[...], b_ref[0]) if b_is_nk else _dot(a_ref[...], b_ref[0])
    if has_res:
        acc = acc + r_ref[...]
    o_ref[...] = acc.astype(o_ref.dtype)


def _matmul(a, b, layer, *, tm, tn, n=None, res=None, out_dtype=F32, b_is_nk=False, name="matmul"):
    M, K = a.shape
    N = n if n is not None else (b.shape[1] if b_is_nk else b.shape[2])
    if b_is_nk:
        b_spec = pl.BlockSpec((1, tn, K), lambda j, i: (layer, j, 0))
    else:
        b_spec = pl.BlockSpec((1, K, tn), lambda j, i: (layer, 0, j))
    in_specs = [pl.BlockSpec((tm, K), lambda j, i: (i, 0)), b_spec]
    args = [a, b]
    if res is not None:
        in_specs.append(pl.BlockSpec((tm, tn), lambda j, i: (i, j)))
        args.append(res)
    return pl.pallas_call(
        functools.partial(_mm_kernel, has_res=res is not None, b_is_nk=b_is_nk),
        grid=(N // tn, M // tm),
        in_specs=in_specs,
        out_specs=pl.BlockSpec((tm, tn), lambda j, i: (i, j)),
        out_shape=jax.ShapeDtypeStruct((M, N), out_dtype),
        compiler_params=_cp("parallel", "parallel"),
        name=name,
    )(*args)


def _ret_core(q, k, v, s_prev, decay, qdec):
    qb = q.astype(BF16)
    inner = _dot_nt(qb, k.astype(BF16)) * decay
    o = _dot(inner.astype(BF16), v.astype(BF16))
    return o, qb


def _ret_epilogue(o, gn, gate):
    mu = jnp.mean(o, axis=-1, keepdims=True)
    d = o - mu
    var = jnp.mean(d * d, axis=-1, keepdims=True)
    y = d * lax.rsqrt(var + EPS) * gn
    return y * (gate * jax.nn.sigmoid(gate))


def _ret_prompt_kernel(lg_ref, q_ref, k_ref, v_ref, gr_ref, cos_ref, sin_ref, gn_ref,
                       o_ref, sfin_ref, st_ref, *, C, DK):
    h = pl.program_id(1)
    c = pl.program_id(2)
    lg = lg_ref[h]

    @pl.when(c == 0)
    def _():
        st_ref[...] = jnp.zeros_like(st_ref)

    cos = cos_ref[...]
    sin = sin_ref[...]
    q = q_ref[...]
    k = k_ref[...]
    q = q * cos + pltpu.roll(q, DK // 2, 1) * sin
    k = (k * cos + pltpu.roll(k, DK // 2, 1) * sin) * (DK ** -0.5)
    v = v_ref[...]
    r = lax.broadcasted_iota(I32, (C, 1), 0)
    col = lax.broadcasted_iota(I32, (1, C), 1)
    diff = r - col
    decay = jnp.where(diff >= 0, jnp.exp(lg * jnp.maximum(diff, 0).astype(F32)), 0.0)
    s_prev = st_ref[...]
    o, qb = _ret_core(q, k, v, s_prev, decay, None)
    o = o + _dot(qb, s_prev.astype(BF16)) * jnp.exp(lg * (r + 1).astype(F32))
    kd = k * jnp.exp(lg * (C - 1 - r).astype(F32))
    s_new = jnp.exp(lg * C) * s_prev + _dot(kd.T.astype(BF16), v.astype(BF16))
    st_ref[...] = s_new
    sfin_ref[0, 0] = s_new
    o_ref[...] = _ret_epilogue(o, gn_ref[...], gr_ref[...]).astype(o_ref.dtype)


def _ret_sample_kernel(lg_ref, q_ref, k_ref, v_ref, gr_ref, cos_ref, sin_ref, gn_ref, s_ref,
                       o_ref, snew_ref, *, NB, LD, DK):
    h = pl.program_id(1)
    lg = lg_ref[h]
    L = NB * LD
    cos = cos_ref[...]
    sin = sin_ref[...]
    q = q_ref[...]
    k = k_ref[...]
    q = q * cos + pltpu.roll(q, DK // 2, 1) * sin
    k = (k * cos + pltpu.roll(k, DK // 2, 1) * sin) * (DK ** -0.5)
    v = v_ref[...]
    r = lax.broadcasted_iota(I32, (L, 1), 0)
    col = lax.broadcasted_iota(I32, (1, L), 1)
    rb = _idiv(r, LD)
    rt = r - rb * LD
    same = rb == _idiv(col, LD)
    diff = r - col
    decay = jnp.where(same & (diff >= 0), jnp.exp(lg * jnp.maximum(diff, 0).astype(F32)), 0.0)
    o, qb = _ret_core(q, k, v, None, decay, None)
    qdec = jnp.exp(lg * (rt + 1).astype(F32))
    kd = k * jnp.exp(lg * (LD - 1 - rt).astype(F32))
    vb = v.astype(BF16)
    sdec = jnp.exp(lg * LD)
    for s in range(NB):
        mine = rb == s
        s_prev = s_ref[0, s, 0]
        o = o + jnp.where(mine, _dot(qb, s_prev.astype(BF16)) * qdec, 0.0)
        kds = jnp.where(mine, kd, 0.0)
        snew_ref[s, 0] = sdec * s_prev + _dot(kds.T.astype(BF16), vb)
    o_ref[...] = _ret_epilogue(o, gn_ref[...], gr_ref[...]).astype(o_ref.dtype)


def _retention(zA, lg, gn, tabs_p, tabs_s, state, layer, dm):
    H, DK, DV, B, S, Bd, LD, Tp = dm["H"], dm["DK"], dm["DV"], dm["B"], dm["S"], dm["Bd"], dm["LD"], dm["Tp"]
    C = _tile(S, 256)
    nc = S // C
    gn2 = gn.reshape(1, H * DV)
    o_p, s_p = pl.pallas_call(
        functools.partial(_ret_prompt_kernel, C=C, DK=DK),
        grid_spec=pltpu.PrefetchScalarGridSpec(
            num_scalar_prefetch=1,
            grid=(B, H, nc),
            in_specs=[
                pl.BlockSpec((C, DK), lambda b, h, c, lg: (b * nc + c, h)),
                pl.BlockSpec((C, DK), lambda b, h, c, lg: (b * nc + c, H + h)),
                pl.BlockSpec((C, DV), lambda b, h, c, lg: (b * nc + c, 2 * H + h)),
                pl.BlockSpec((C, DV), lambda b, h, c, lg: (b * nc + c, 3 * H + h)),
                pl.BlockSpec((C, DK), lambda b, h, c, lg: (c, 0)),
                pl.BlockSpec((C, DK), lambda b, h, c, lg: (c, 0)),
                pl.BlockSpec((1, DV), lambda b, h, c, lg: (0, h)),
            ],
            out_specs=[
                pl.BlockSpec((C, DV), lambda b, h, c, lg: (b * nc + c, h)),
                pl.BlockSpec((1, 1, DK, DV), lambda b, h, c, lg: (b, h, 0, 0)),
            ],
            scratch_shapes=[pltpu.VMEM((DK, DV), F32)],
        ),
        out_shape=[jax.ShapeDtypeStruct((Tp, H * DV), BF16),
                   jax.ShapeDtypeStruct((B, H, DK, DV), F32)],
        compiler_params=_cp("parallel", "parallel", "arbitrary"),
        name="retention_prompt",
    )(lg, zA, zA, zA, zA, tabs_p[0], tabs_p[1], gn2)

    NB = _tile(Bd, 8, 1)
    L = NB * LD
    off = Tp // L
    o_s, s_s = pl.pallas_call(
        functools.partial(_ret_sample_kernel, NB=NB, LD=LD, DK=DK),
        grid_spec=pltpu.PrefetchScalarGridSpec(
            num_scalar_prefetch=1,
            grid=(Bd // NB, H),
            in_specs=[
                pl.BlockSpec((L, DK), lambda i, h, lg: (off + i, h)),
                pl.BlockSpec((L, DK), lambda i, h, lg: (off + i, H + h)),
                pl.BlockSpec((L, DV), lambda i, h, lg: (off + i, 2 * H + h)),
                pl.BlockSpec((L, DV), lambda i, h, lg: (off + i, 3 * H + h)),
                pl.BlockSpec((L, DK), lambda i, h, lg: (0, 0)),
                pl.BlockSpec((L, DK), lambda i, h, lg: (0, 0)),
                pl.BlockSpec((1, DV), lambda i, h, lg: (0, h)),
                pl.BlockSpec((1, NB, 1, DK, DV), lambda i, h, lg: (layer, i, h, 0, 0)),
            ],
            out_specs=[
                pl.BlockSpec((L, DV), lambda i, h, lg: (i, h)),
                pl.BlockSpec((NB, 1, DK, DV), lambda i, h, lg: (i, h, 0, 0)),
            ],
        ),
        out_shape=[jax.ShapeDtypeStruct((Bd * LD, H * DV), BF16),
                   jax.ShapeDtypeStruct((Bd, H, DK, DV), F32)],
        compiler_params=_cp("parallel", "parallel"),
        name="retention_sample",
    )(lg, zA, zA, zA, zA, tabs_s[0], tabs_s[1], gn2, state)
    return jnp.concatenate([o_p, o_s], axis=0), s_p, s_s


def _pool_prompt_kernel(u_ref, halo_ref, w_ref, sc_ref, o_ref, full_ref, *, TL, HALO, GC):
    i = pl.program_id(1)
    u = u_ref[...]
    full_ref[pl.ds(HALO, TL), :] = u
    full_ref[pl.ds(0, HALO), :] = jnp.where(i == 0, 0.0, halo_ref[...])
    pos = i * TL + lax.broadcasted_iota(I32, (TL, 1), 0)
    for g, w in enumerate(POOL_WINDOWS):
        cs = slice(g * GC, (g + 1) * GC)
        win = full_ref[pl.ds(HALO, TL), cs]
        for j in range(1, w):
            win = win + full_ref[pl.ds(HALO - j, TL), cs]
        cnt = jnp.minimum(w, pos + 1).astype(F32)
        pooled = win / cnt - u[:, cs]
        z = _dot(pooled.astype(BF16), w_ref[g])
        o_ref[:, cs] = (z * sc_ref[:, cs]).astype(o_ref.dtype)


def _pool_sample_kernel(u_ref, buf_ref, w_ref, sc_ref, o_ref, nbuf_ref, *, LD, PB, GC, pos0):
    rows = [buf_ref[0, j] for j in range(PB)] + [u_ref[:, t, :] for t in range(LD)]
    for j in range(PB):
        nbuf_ref[j] = rows[LD + j]
    for g, w in enumerate(POOL_WINDOWS):
        cs = slice(g * GC, (g + 1) * GC)
        pooled = []
        for t in range(LD):
            win = rows[PB + t][:, cs]
            for j in range(1, w):
                win = win + rows[PB + t - j][:, cs]
            cnt = float(min(w, pos0 + t + 1))
            pooled.append(win / cnt - rows[PB + t][:, cs])
        z = _dot(jnp.concatenate(pooled, axis=0).astype(BF16), w_ref[g])
        nb = u_ref.shape[0]
        for t in range(LD):
            o_ref[:, t, cs] = (z[t * nb:(t + 1) * nb] * sc_ref[:, cs]).astype(o_ref.dtype)


def _pooling(zA, pool_w, pool_scale, buf, layer, dm):
    B, S, Bd, LD, Tp, PW, PB = dm["B"], dm["S"], dm["Bd"], dm["LD"], dm["Tp"], dm["PW"], dm["PB"]
    G = len(POOL_WINDOWS)
    GC = PW // G
    HALO = 16
    assert PB < HALO and S % HALO == 0 and max(POOL_WINDOWS) <= HALO
    TL = _tile(S, 512, HALO)
    nt = S // TL
    ucol = dm["u_off"] // PW
    wb = pool_w.astype(BF16)
    sc = pool_scale.reshape(1, PW)
    o_p = pl.pallas_call(
        functools.partial(_pool_prompt_kernel, TL=TL, HALO=HALO, GC=GC),
        grid=(B, nt),
        in_specs=[
            pl.BlockSpec((TL, PW), lambda b, i: (b * nt + i, ucol)),
            pl.BlockSpec((HALO, PW), lambda b, i: (jnp.maximum((b * nt + i) * (TL // HALO) - 1, 0), ucol)),
            pl.BlockSpec((G, GC, GC), lambda b, i: (0, 0, 0)),
            pl.BlockSpec((1, PW), lambda b, i: (0, 0)),
        ],
        out_specs=pl.BlockSpec((TL, PW), lambda b, i: (b * nt + i, 0)),
        out_shape=jax.ShapeDtypeStruct((Tp, PW), BF16),
        scratch_shapes=[pltpu.VMEM((TL + HALO, PW), F32)],
        compiler_params=_cp("parallel", "parallel"),
        name="pool_prompt",
    )(zA, zA, wb, sc)

    u_s = zA[Tp:, dm["u_off"]:dm["u_off"] + PW].reshape(Bd, LD, PW)
    NB = _tile(Bd, 32, 8)
    o_s, nbuf = pl.pallas_call(
        functools.partial(_pool_sample_kernel, LD=LD, PB=PB, GC=GC, pos0=dm["P0"]),
        grid=(Bd // NB,),
        in_specs=[
            pl.BlockSpec((NB, LD, PW), lambda i: (i, 0, 0)),
            pl.BlockSpec((1, PB, NB, PW), lambda i: (layer, 0, i, 0)),
            pl.BlockSpec((G, GC, GC), lambda i: (0, 0, 0)),
            pl.BlockSpec((1, PW), lambda i: (0, 0)),
        ],
        out_specs=[pl.BlockSpec((NB, LD, PW), lambda i: (i, 0, 0)),
                   pl.BlockSpec((PB, NB, PW), lambda i: (0, i, 0))],
        out_shape=[jax.ShapeDtypeStruct((Bd, LD, PW), BF16), jax.ShapeDtypeStruct((PB, Bd, PW), F32)],
        compiler_params=_cp("parallel"),
        name="pool_sample",
    )(u_s, buf, wb, sc)
    pool_p = zA[:Tp, dm["u_off"]:dm["u_off"] + PW].reshape(B, S, PW)[:, S - PB:]
    return jnp.concatenate([o_p, o_s.reshape(Bd * LD, PW)], axis=0), pool_p, nbuf


def _qprep_kernel(cq_ref, g_ref, w_ref, cos_ref, sa_ref, sb_ref, gn_ref, gr_ref, q_ref,
                  *, MH, NOPE, ROPE, scale):
    x = cq_ref[...]
    y = x * lax.rsqrt(jnp.mean(x * x, axis=-1, keepdims=True) + EPS) * g_ref[...]
    qq = _dot(y.astype(BF16), w_ref[...])
    qn = qq[:, :MH * NOPE]
    qr = qq[:, MH * NOPE:]
    W = MH * ROPE
    qr = (qr * cos_ref[...] + pltpu.roll(qr, W - ROPE // 2, 1) * sa_ref[...]
          + pltpu.roll(qr, ROPE // 2, 1) * sb_ref[...])
    gn = gn_ref[...]
    gr = gr_ref[...]
    for h in range(MH):
        a = qn[:, h * NOPE:(h + 1) * NOPE]
        b = qr[:, h * ROPE:(h + 1) * ROPE]
        ss = jnp.sum(a * a, axis=-1, keepdims=True) + jnp.sum(b * b, axis=-1, keepdims=True)
        inv = lax.rsqrt(ss * (1.0 / (NOPE + ROPE)) + EPS) * scale
        q_ref[h, :, :NOPE] = (a * inv * gn).astype(q_ref.dtype)
        q_ref[h, :, NOPE:] = (b * inv * gr).astype(q_ref.dtype)


def _kvprep_kernel(ckv_ref, kpe_ref, g_ref, wuk_ref, wuv_ref, cos_ref, sa_ref, sb_ref,
                   rows_ref, k_ref, v_ref, *, MH, NOPE, ROPE, KV):
    x = ckv_ref[...]
    lat = x * lax.rsqrt(jnp.mean(x * x, axis=-1, keepdims=True) + EPS) * g_ref[...]
    kp = kpe_ref[...]
    W = kp.shape[1]
    kp = (kp * cos_ref[...] + pltpu.roll(kp, W - ROPE // 2, 1) * sa_ref[...]
          + pltpu.roll(kp, ROPE // 2, 1) * sb_ref[...])
    rows_ref[:, :KV] = lat
    rows_ref[:, KV:] = kp[:, :ROPE]
    latb = lat.astype(BF16)
    kn = _dot(latb, wuk_ref[...])
    v_ref[...] = _dot(latb, wuv_ref[...]).astype(v_ref.dtype)
    sspe = jnp.sum(kp * kp, axis=-1, keepdims=True)
    for h in range(MH):
        a = kn[:, h * NOPE:(h + 1) * NOPE]
        ss = jnp.sum(a * a, axis=-1, keepdims=True) + sspe
        inv = lax.rsqrt(ss * (1.0 / (NOPE + ROPE)) + EPS)
        k_ref[h, :, :NOPE] = (a * inv).astype(k_ref.dtype)
        k_ref[h, :, NOPE:] = (kp[:, :ROPE] * inv).astype(k_ref.dtype)


def _attn_prompt_kernel(q_ref, k_ref, v_ref, o_ref, *, TQ):
    i = pl.program_id(2)
    q = q_ref[0]
    qpos = i * TQ + lax.broadcasted_iota(I32, (TQ, 1), 0)

    def body(j, carry):
        m, l, acc = carry
        start = pl.multiple_of(j * TQ, TQ)
        s = _dot_nt(q, k_ref[0, pl.ds(start, TQ), :])
        kpos = start + lax.broadcasted_iota(I32, (1, TQ), 1)
        s = jnp.where(kpos <= qpos, s, NEG_INF)
        m_new = jnp.maximum(m, jnp.max(s, axis=-1, keepdims=True))
        corr = jnp.exp(m - m_new)
        p = jnp.exp(s - m_new)
        l = l * corr + jnp.sum(p, axis=-1, keepdims=True)
        acc = acc * corr + _dot(p.astype(BF16), v_ref[pl.ds(start, TQ), :])
        return m_new, l, acc

    init = (jnp.full((TQ, 1), NEG_INF, F32), jnp.zeros((TQ, 1), F32), jnp.zeros(o_ref.shape, F32))
    _, l, acc = lax.fori_loop(0, i + 1, body, init)
    o_ref[...] = (acc / l).astype(o_ref.dtype)


def _attn_sample_kernel(pt_ref, qn_ref, qr_ref, self_ref, cache_hbm, wukt_ref, wuv_ref, o_ref,
                        qc_ref, m_ref, l_ref, acc_ref, pbuf, sem,
                        *, PP, MH, NOPE, ROPE, KV, LD, NP_STEPS, NB, layer):
    b = pl.program_id(0)
    step = pl.program_id(1)
    NQ = LD * MH
    n = b * NP_STEPS + step
    slot = lax.rem(n, PAGE_SLOTS)

    def page_copy(bb, ss, sl, i):
        return pltpu.make_async_copy(cache_hbm.at[layer, pt_ref[bb, ss * PP + i]], pbuf.at[sl, i], sem.at[sl])

    def fetch(ahead):
        wrap = step + ahead >= NP_STEPS
        bb = jnp.where(wrap, b + 1, b)
        ss = jnp.where(wrap, step + ahead - NP_STEPS, step + ahead)
        sl = lax.rem(n + ahead, PAGE_SLOTS)
        for i in range(PP):
            page_copy(bb, ss, sl, i).start()

    @pl.when(n == 0)
    def _():
        for ahead in range(PAGE_SLOTS - 1):
            fetch(ahead)

    @pl.when(n + PAGE_SLOTS - 1 < NB * NP_STEPS)
    def _():
        fetch(PAGE_SLOTS - 1)

    for i in range(PP):
        page_copy(b, step, slot, i).wait()

    def absorb(chunks, mask):
        s_parts, lat_parts = [], []
        for page_t in chunks:
            W = page_t.shape[1]
            pb = page_t.astype(BF16)
            latb = pb[:KV]
            both = _dot(qc_ref[...], latb)
            kn = both[:MH * NOPE]
            kpe = page_t[KV:]
            ss = jnp.sum((kn * kn).reshape(MH, NOPE, W), axis=1) + jnp.sum(kpe * kpe, axis=0, keepdims=True)
            rinv = lax.rsqrt(ss * (1.0 / (NOPE + ROPE)) + EPS)
            s = both[MH * NOPE:] + _dot(qr_ref[0], pb[KV:])
            s_parts.append(s * jnp.concatenate([rinv] * LD, axis=0))
            lat_parts.append(latb)
        s = s_parts[0] if len(s_parts) == 1 else jnp.concatenate(s_parts, axis=1)
        if mask is not None:
            s = jnp.where(mask, s, NEG_INF)
        m_old = m_ref[...]
        m_new = jnp.maximum(m_old, jnp.max(s, axis=-1, keepdims=True))
        corr = jnp.exp(m_old - m_new)
        p = jnp.exp(s - m_new)
        l_ref[...] = l_ref[...] * corr + jnp.sum(p, axis=-1, keepdims=True)
        pb16 = p.astype(BF16)
        pv, off = None, 0
        for latb in lat_parts:
            W = latb.shape[1]
            part = _dot_nt(pb16[:, off:off + W], latb)
            pv = part if pv is None else pv + part
            off += W
        acc_ref[...] = acc_ref[...] * corr + pv
        m_ref[...] = m_new

    @pl.when(step == 0)
    def _():
        m_ref[...] = jnp.full_like(m_ref, NEG_INF)
        l_ref[...] = jnp.zeros_like(l_ref)
        acc_ref[...] = jnp.zeros_like(acc_ref)
        qc_ref[pl.ds(0, MH * NOPE), :] = wukt_ref[...]
        qc_ref[pl.ds(MH * NOPE, NQ), :] = _dot(qn_ref[0], wukt_ref[...]).astype(BF16)
        page_t = self_ref[0]
        t = _idiv(lax.broadcasted_iota(I32, (NQ, 1), 0), MH)
        j = lax.broadcasted_iota(I32, (1, page_t.shape[1]), 1)
        absorb([page_t], j <= t)

    if PP % 2 == 0:
        absorb([jnp.concatenate([pbuf[slot, c], pbuf[slot, c + 1]], axis=1) for c in range(0, PP, 2)], None)
    else:
        absorb([pbuf[slot, c] for c in range(PP)], None)

    @pl.when(step == NP_STEPS - 1)
    def _():
        o_lat = (acc_ref[...] / l_ref[...]).astype(BF16)
        VD = wuv_ref.shape[1] // MH
        hrow = lax.broadcasted_iota(I32, (MH, MH * VD), 0)
        hcol = _idiv(lax.broadcasted_iota(I32, (MH, MH * VD), 1), VD)
        for t in range(LD):
            full = _dot(o_lat[t * MH:(t + 1) * MH], wuv_ref[...])
            o_ref[0, pl.ds(t, 1), :] = jnp.sum(jnp.where(hrow == hcol, full, 0.0), axis=0,
                                              keepdims=True).astype(o_ref.dtype)


def _mla(zA, lw, tabs_q, tabs_k, cache_t, layer, page_table, dm):
    T, Tp, B, S, Bd, LD = dm["T"], dm["Tp"], dm["B"], dm["S"], dm["Bd"], dm["LD"]
    MH, NOPE, ROPE, KV, QL, VD = dm["MH"], dm["NOPE"], dm["ROPE"], dm["KV"], dm["QL"], dm["VD"]
    QK = NOPE + ROPE
    ROW = KV + ROPE
    scale = QK ** -0.5
    tm = _tile(T, 512)

    w_uq = lw["mla_w_uq"].reshape(QL, MH, QK)
    w_uq = jnp.concatenate([w_uq[:, :, :NOPE].reshape(QL, MH * NOPE),
                            w_uq[:, :, NOPE:].reshape(QL, MH * ROPE)], axis=1).astype(BF16)
    w_ukv = lw["mla_w_ukv"].reshape(KV, MH, NOPE + VD)
    w_uk = w_ukv[:, :, :NOPE].reshape(KV, MH * NOPE).astype(BF16)
    w_uv = w_ukv[:, :, NOPE:].reshape(KV, MH * VD).astype(BF16)
    gq, gk = lw["mla_qn_g"], lw["mla_kn_g"]
    gn = (gq[:NOPE] * gk[:NOPE]).reshape(1, NOPE)
    gr = jnp.tile(gq[NOPE:] * gk[NOPE:], 2).reshape(1, ROPE)

    c_q = zA[:, dm["cq_off"]:dm["cq_off"] + QL]
    q = pl.pallas_call(
        functools.partial(_qprep_kernel, MH=MH, NOPE=NOPE, ROPE=ROPE, scale=scale),
        grid=(T // tm,),
        in_specs=[
            pl.BlockSpec((tm, QL), lambda i: (i, 0)),
            pl.BlockSpec((1, QL), lambda i: (0, 0)),
            pl.BlockSpec((QL, MH * QK), lambda i: (0, 0)),
            pl.BlockSpec((tm, MH * ROPE), lambda i: (i, 0)),
            pl.BlockSpec((tm, MH * ROPE), lambda i: (i, 0)),
            pl.BlockSpec((tm, MH * ROPE), lambda i: (i, 0)),
            pl.BlockSpec((1, NOPE), lambda i: (0, 0)),
            pl.BlockSpec((1, ROPE), lambda i: (0, 0)),
        ],
        out_specs=pl.BlockSpec((MH, tm, QK), lambda i: (0, i, 0)),
        out_shape=jax.ShapeDtypeStruct((MH, T, QK), BF16),
        compiler_params=_cp("parallel"),
        name="mla_q_prep",
    )(c_q, lw["mla_qnorm_g"].reshape(1, QL), w_uq, tabs_q[0], tabs_q[1], tabs_q[2], gn, gr)

    kvb = dm["kv_off"] // KV
    kpb = dm["kpe_off"] // (2 * ROPE)
    rows, k, v = pl.pallas_call(
        functools.partial(_kvprep_kernel, MH=MH, NOPE=NOPE, ROPE=ROPE, KV=KV),
        grid=(T // tm,),
        in_specs=[
            pl.BlockSpec((tm, KV), lambda i: (i, kvb)),
            pl.BlockSpec((tm, 2 * ROPE), lambda i: (i, kpb)),
            pl.BlockSpec((1, KV), lambda i: (0, 0)),
            pl.BlockSpec((KV, MH * NOPE), lambda i: (0, 0)),
            pl.BlockSpec((KV, MH * VD), lambda i: (0, 0)),
            pl.BlockSpec((tm, 2 * ROPE), lambda i: (i, 0)),
            pl.BlockSpec((tm, 2 * ROPE), lambda i: (i, 0)),
            pl.BlockSpec((tm, 2 * ROPE), lambda i: (i, 0)),
        ],
        out_specs=[
            pl.BlockSpec((tm, ROW), lambda i: (i, 0)),
            pl.BlockSpec((MH, tm, QK), lambda i: (0, i, 0)),
            pl.BlockSpec((tm, MH * VD), lambda i: (i, 0)),
        ],
        out_shape=[jax.ShapeDtypeStruct((T, ROW), F32),
                   jax.ShapeDtypeStruct((MH, T, QK), BF16),
                   jax.ShapeDtypeStruct((T, MH * VD), BF16)],
        compiler_params=_cp("parallel"),
        name="mla_kv_prep",
    )(zA, zA, lw["mla_kvnorm_g"].reshape(1, KV), w_uk, w_uv, tabs_k[0], tabs_k[1], tabs_k[2])

    TQ = _tile(S, 512)
    nq = S // TQ
    o_p = pl.pallas_call(
        functools.partial(_attn_prompt_kernel, TQ=TQ),
        grid=(B, MH, nq),
        in_specs=[
            pl.BlockSpec((1, TQ, QK), lambda b, h, i: (h, b * nq + i, 0)),
            pl.BlockSpec((1, S, QK), lambda b, h, i: (h, b, 0)),
            pl.BlockSpec((S, VD), lambda b, h, i: (b, h)),
        ],
        out_specs=pl.BlockSpec((TQ, VD), lambda b, h, i: (b * nq + i, h)),
        out_shape=jax.ShapeDtypeStruct((Tp, MH * VD), BF16),
        compiler_params=_cp("parallel", "parallel", "parallel"),
        name="mla_attn_prompt",
    )(q, k, v)

    PAGE = cache_t.shape[3]
    NP = page_table.shape[1]
    PP = _tile(NP, 16, 1)
    NQ = LD * MH
    qs = q[:, Tp:].reshape(MH, Bd, LD, QK).transpose(1, 2, 0, 3).reshape(Bd, NQ, QK)
    head_of_row = jnp.arange(NQ) % MH
    eye = (head_of_row[:, None] == jnp.arange(MH)[None, :])
    qn_bd = jnp.where(eye[None, :, :, None], qs[:, :, None, :NOPE], 0).reshape(Bd, NQ, MH * NOPE)
    qr_s = qs[:, :, NOPE:]
    self_t = jnp.pad(rows[Tp:].reshape(Bd, LD, ROW).transpose(0, 2, 1), ((0, 0), (0, 0), (0, PAGE - LD)))
    assert NP // PP >= PAGE_SLOTS - 1
    o_s = pl.pallas_call(
        functools.partial(_attn_sample_kernel, PP=PP, MH=MH, NOPE=NOPE, ROPE=ROPE, KV=KV, LD=LD,
                          NP_STEPS=NP // PP, NB=Bd, layer=layer),
        grid_spec=pltpu.PrefetchScalarGridSpec(
            num_scalar_prefetch=1,
            grid=(Bd, NP // PP),
            in_specs=[
                pl.BlockSpec((1, NQ, MH * NOPE), lambda b, s, pt: (b, 0, 0)),
                pl.BlockSpec((1, NQ, ROPE), lambda b, s, pt: (b, 0, 0)),
                pl.BlockSpec((1, ROW, PAGE), lambda b, s, pt: (b, 0, 0)),
                pl.BlockSpec(memory_space=pl.ANY),
                pl.BlockSpec((MH * NOPE, KV), lambda b, s, pt: (0, 0)),
                pl.BlockSpec((KV, MH * VD), lambda b, s, pt: (0, 0)),
            ],
            out_specs=pl.BlockSpec((1, LD, MH * VD), lambda b, s, pt: (b, 0, 0)),
            scratch_shapes=[pltpu.VMEM((MH * NOPE + NQ, KV), BF16), pltpu.VMEM((NQ, 1), F32), pltpu.VMEM((NQ, 1), F32),
                            pltpu.VMEM((NQ, KV), F32), pltpu.VMEM((PAGE_SLOTS, PP, ROW, PAGE), F32),
                            pltpu.SemaphoreType.DMA((PAGE_SLOTS,))],
        ),
        out_shape=jax.ShapeDtypeStruct((Bd, LD, MH * VD), BF16),
        compiler_params=_cp("arbitrary", "arbitrary"),
        name="mla_attn_sample",
    )(page_table, qn_bd, qr_s, self_t, cache_t, w_uk.T, w_uv)
    o_m = jnp.concatenate([o_p, o_s.reshape(Bd * LD, MH * VD)], axis=0)
    return o_m, rows


def _merge_kernel(or_ref, op_ref, om_ref, wr_ref, wp_ref, wm_ref, g0_ref, g1_ref, g2_ref, o_ref):
    acc = jax.nn.sigmoid(g0_ref[...]) * _dot(or_ref[...], wr_ref[0])
    acc = acc + jax.nn.sigmoid(g1_ref[...]) * _dot(op_ref[...], wp_ref[0])
    acc = acc + jax.nn.sigmoid(g2_ref[...]) * _dot(om_ref[...], wm_ref[0])
    o_ref[...] = acc.astype(o_ref.dtype)


def _merge(o_r, o_p, o_m, w_r, w_p, w_m, layer, gl, dm):
    T, D = dm["T"], dm["D"]
    tm = _tile(T, 512)
    tn = _tile(D, 1024, LANES)
    nj = D // tn
    branch = lambda w: pl.BlockSpec((tm, w.shape[1]), lambda j, i: (i, 0))
    weight = lambda w: pl.BlockSpec((1, w.shape[1], tn), lambda j, i: (layer, 0, j))
    gate = lambda n: pl.BlockSpec((tm, tn), lambda j, i: (i, n * nj + j))
    return pl.pallas_call(
        _merge_kernel,
        grid=(nj, T // tm),
        in_specs=[branch(w_r), branch(w_p), branch(w_m), weight(w_r), weight(w_p), weight(w_m),
                  gate(0), gate(1), gate(2)],
        out_specs=pl.BlockSpec((tm, tn), lambda j, i: (i, j)),
        out_shape=jax.ShapeDtypeStruct((T, D), BF16),
        compiler_params=_cp("parallel", "parallel"),
        name="gated_merge",
    )(o_r, o_p, o_m, w_r, w_p, w_m, gl, gl, gl)


def _router_kernel(x_ref, g_ref, w_ref, b_ref, h_ref, r_ref):
    x = x_ref[...]
    h = x * lax.rsqrt(jnp.mean(x * x, axis=-1, keepdims=True) + EPS) * g_ref[...]
    half = x.shape[1] // 2
    bits = lax.bitcast_convert_type(h.astype(BF16).astype(F32), jnp.uint32)
    h_ref[...] = (bits[:, :half] >> 16) | (bits[:, half:] & jnp.uint32(0xFFFF0000))
    logits = jnp.dot(h, w_ref[...], preferred_element_type=F32, precision=lax.Precision.HIGHEST) + b_ref[...]
    lane = lax.broadcasted_iota(I32, logits.shape, 1)
    big = jnp.int32(1 << 20)
    gl = jnp.where(lane < N_GROUPS, logits, -jnp.inf)
    gmax = jnp.max(gl, axis=-1, keepdims=True)
    g_idx = jnp.min(jnp.where(gl == gmax, lane, big), axis=-1, keepdims=True)
    g_w = 1.0 / jnp.sum(jnp.exp(gl - gmax), axis=-1, keepdims=True)
    lo = N_GROUPS + g_idx * EXPERTS_PER_GROUP
    sel = (lane >= lo) & (lane < lo + EXPERTS_PER_GROUP)
    el = jnp.where(sel, logits, -jnp.inf)
    pe = jnp.exp(el - jnp.max(el, axis=-1, keepdims=True))
    p = pe / jnp.sum(pe, axis=-1, keepdims=True)
    p = jnp.where(sel, p, -1.0)
    p1 = jnp.max(p, axis=-1, keepdims=True)
    i1 = jnp.min(jnp.where(p == p1, lane, big), axis=-1, keepdims=True)
    prest = jnp.where(lane == i1, -1.0, p)
    p2 = jnp.max(prest, axis=-1, keepdims=True)
    i2 = jnp.min(jnp.where(prest == p2, lane, big), axis=-1, keepdims=True)
    den = p1 + p2
    w1 = g_w * p1 / den
    w2 = g_w * p2 / den
    out = jnp.where(lane == 0, (i1 - N_GROUPS).astype(F32),
                    jnp.where(lane == 1, (i2 - N_GROUPS).astype(F32),
                              jnp.where(lane == 2, w1, jnp.where(lane == 3, w2, 0.0))))
    r_ref[...] = out


def _moe_up_kernel(te_ref, tok_ref, nu_ref, h_hbm, wg_ref, wu_ref, a_ref, xbuf, sem, *, TM):
    i = pl.program_id(0)
    n_used = nu_ref[0]
    slot = i % 2

    def row_copy(s, r, tok):
        return pltpu.make_async_copy(h_hbm.at[pl.ds(tok, 1)], xbuf.at[s, pl.ds(r, 1)], sem.at[s])

    def gather(tile, s):
        def issue(r, c):
            row_copy(s, r, tok_ref[tile * TM + r]).start()
            return c

        lax.fori_loop(0, TM, issue, 0)

    @pl.when((i == 0) & (n_used > 0))
    def _():
        gather(0, 0)

    @pl.when(i + 1 < n_used)
    def _():
        gather(i + 1, 1 - slot)

    @pl.when(i < n_used)
    def _():
        def drain(r, c):
            row_copy(slot, r, 0).wait()
            return c

        lax.fori_loop(0, TM, drain, 0)
        w = xbuf[slot]
        x = jnp.concatenate([lax.bitcast_convert_type(w << 16, F32),
                             lax.bitcast_convert_type(w & jnp.uint32(0xFFFF0000), F32)], axis=1).astype(BF16)
        g = _dot(x, wg_ref[0, 0])
        u = _dot(x, wu_ref[0, 0])
        a_ref[...] = (g * jax.nn.sigmoid(g) * u).astype(a_ref.dtype)

    @pl.when(i >= n_used)
    def _():
        a_ref[...] = jnp.zeros_like(a_ref)


def _moe_down_kernel(te_ref, nu_ref, a_ref, wd_ref, rw_ref, y_ref):
    i = pl.program_id(0)

    @pl.when(i < nu_ref[0])
    def _():
        y_ref[...] = _dot(a_ref[...], wd_ref[0, 0]) * rw_ref[...]

    @pl.when(i >= nu_ref[0])
    def _():
        y_ref[...] = jnp.zeros_like(y_ref)


def _moe_combine_kernel(p1_ref, p2_ref, x_ref, y_hbm, o_ref, b1, b2, sem, *, TM, NT):
    i = pl.program_id(0)
    slot = i % 2

    def row_copy(s, src_row, dst, r):
        return pltpu.make_async_copy(y_hbm.at[pl.ds(src_row, 1)], dst.at[s, pl.ds(r, 1)], sem.at[s])

    def gather(tile, s):
        def issue(r, c):
            row_copy(s, p1_ref[tile * TM + r], b1, r).start()
            row_copy(s, p2_ref[tile * TM + r], b2, r).start()
            return c

        lax.fori_loop(0, TM, issue, 0)

    @pl.when(i == 0)
    def _():
        gather(0, 0)

    @pl.when(i + 1 < NT)
    def _():
        gather(i + 1, 1 - slot)

    def drain(r, c):
        row_copy(slot, 0, b1, r).wait()
        row_copy(slot, 0, b2, r).wait()
        return c

    lax.fori_loop(0, TM, drain, 0)
    o_ref[...] = x_ref[...] + (b1[slot] + b2[slot])


def _moe(x1, norm_g, wr, br, wg, wu, wd, layer, dm):
    T, D = dm["T"], dm["D"]
    E, DE = wg.shape[1], wg.shape[3]
    tm = _tile(T, 256)
    h2, route = pl.pallas_call(
        _router_kernel,
        grid=(T // tm,),
        in_specs=[
            pl.BlockSpec((tm, D), lambda i: (i, 0)),
            pl.BlockSpec((1, D), lambda i: (0, 0)),
            pl.BlockSpec((D, ROUTE_LANES), lambda i: (0, 0)),
            pl.BlockSpec((1, ROUTE_LANES), lambda i: (0, 0)),
        ],
        out_specs=[pl.BlockSpec((tm, D // 2), lambda i: (i, 0)), pl.BlockSpec((tm, ROUTE_LANES), lambda i: (i, 0))],
        out_shape=[jax.ShapeDtypeStruct((T, D // 2), jnp.uint32), jax.ShapeDtypeStruct((T, ROUTE_LANES), F32)],
        compiler_params=_cp("parallel"),
        name="moe_router",
    )(x1, norm_g.reshape(1, D), wr, br)

    TM = MOE_TM
    n_tiles = pl.cdiv(2 * T, TM) + E
    P = n_tiles * TM
    ids = route[:, :2].astype(I32).reshape(-1)
    wts = route[:, 2:4].reshape(-1)
    onehot = (ids[:, None] == jnp.arange(E, dtype=I32)[None, :]).astype(I32)
    csum = jnp.cumsum(onehot, axis=0)
    rank = jnp.sum(jnp.where(onehot > 0, csum, 0), axis=1) - 1
    counts = csum[-1]
    padded = ((counts + TM - 1) // TM) * TM
    ends = jnp.cumsum(padded)
    starts = ends - padded
    pos = (starts[ids] + rank).astype(I32)
    tok_and_w = jnp.stack([(jnp.arange(2 * T, dtype=I32) // 2).astype(F32), wts], axis=1)
    row_meta = jnp.zeros((P, 2), F32).at[pos].set(tok_and_w)
    row_tok = row_meta[:, 0].astype(I32)
    row_w = row_meta[:, 1:2]
    tile_start = jnp.arange(n_tiles, dtype=I32) * TM
    tile_e = jnp.minimum(jnp.sum((ends[None, :] <= tile_start[:, None]).astype(I32), axis=1), E - 1)
    n_used = (ends[-1] // TM).astype(I32).reshape(1)
    pos2 = pos.reshape(T, 2)

    act = pl.pallas_call(
        functools.partial(_moe_up_kernel, TM=TM),
        grid_spec=pltpu.PrefetchScalarGridSpec(
            num_scalar_prefetch=3,
            grid=(n_tiles,),
            in_specs=[
                pl.BlockSpec(memory_space=pl.ANY),
                pl.BlockSpec((1, 1, D, DE), lambda i, te, tok, nu: (layer, te[i], 0, 0)),
                pl.BlockSpec((1, 1, D, DE), lambda i, te, tok, nu: (layer, te[i], 0, 0)),
            ],
            out_specs=pl.BlockSpec((TM, DE), lambda i, te, tok, nu: (i, 0)),
            scratch_shapes=[pltpu.VMEM((2, TM, D // 2), jnp.uint32), pltpu.SemaphoreType.DMA((2,))],
        ),
        out_shape=jax.ShapeDtypeStruct((P, DE), BF16),
        compiler_params=_cp("arbitrary"),
        name="moe_gather_up",
    )(tile_e, row_tok, n_used, h2, wg, wu)

    ys = pl.pallas_call(
        _moe_down_kernel,
        grid_spec=pltpu.PrefetchScalarGridSpec(
            num_scalar_prefetch=2,
            grid=(n_tiles,),
            in_specs=[
                pl.BlockSpec((TM, DE), lambda i, te, nu: (i, 0)),
                pl.BlockSpec((1, 1, DE, D), lambda i, te, nu: (layer, te[i], 0, 0)),
                pl.BlockSpec((TM, 1), lambda i, te, nu: (i, 0)),
            ],
            out_specs=pl.BlockSpec((TM, D), lambda i, te, nu: (i, 0)),
        ),
        out_shape=jax.ShapeDtypeStruct((P, D), F32),
        compiler_params=_cp("arbitrary"),
        name="moe_down",
    )(tile_e, n_used, act, wd, row_w)

    tc = _tile(T, 256)
    return pl.pallas_call(
        functools.partial(_moe_combine_kernel, TM=tc, NT=T // tc),
        grid_spec=pltpu.PrefetchScalarGridSpec(
            num_scalar_prefetch=2,
            grid=(T // tc,),
            in_specs=[
                pl.BlockSpec((tc, D), lambda i, p1, p2: (i, 0)),
                pl.BlockSpec(memory_space=pl.ANY),
            ],
            out_specs=pl.BlockSpec((tc, D), lambda i, p1, p2: (i, 0)),
            scratch_shapes=[pltpu.VMEM((2, tc, D), F32), pltpu.VMEM((2, tc, D), F32),
                            pltpu.SemaphoreType.DMA((2,))],
        ),
        out_shape=jax.ShapeDtypeStruct((T, D), F32),
        compiler_params=_cp("arbitrary"),
        name="moe_combine",
    )(pos2[:, 0], pos2[:, 1], x1, ys)


def _rope_cos_sin(pos, half):
    inv = ROPE_BASE ** (-jnp.arange(half, dtype=F32) / half)
    ang = pos.astype(F32)[:, None] * inv[None, :]
    return jnp.cos(ang), jnp.sin(ang)


def _ret_tables(pos, dk):
    cos, sin = _rope_cos_sin(pos, dk // 2)
    return jnp.concatenate([cos, cos], axis=1), jnp.concatenate([-sin, sin], axis=1)


def _mla_tables(pos, rope, reps, pad):
    cos, sin = _rope_cos_sin(pos, rope // 2)
    z = jnp.zeros_like(sin)
    tabs = (jnp.concatenate([cos, cos], axis=1), jnp.concatenate([-sin, z], axis=1),
            jnp.concatenate([z, sin], axis=1))
    if pad:
        return tuple(jnp.pad(t, ((0, 0), (0, pad))) for t in tabs)
    return tuple(jnp.tile(t, (1, reps)) for t in tabs)


def kernel(x_prompt, x_sample, cache_mla, page_table, state_ret, state_pool, norm1_g, w_in, ret_gn_g, pool_w, pool_scale, mla_qnorm_g, mla_w_uq, mla_kvnorm_g, mla_w_ukv, mla_qn_g, mla_kn_g, w_br_ret, w_br_pool, w_br_mla, w_out, norm2_g, router_g_w, router_g_b, router_e_w, router_e_b, exp_w_gate, exp_w_up, exp_w_down):
    B, S, D = x_prompt.shape
    Bd, LD, _ = x_sample.shape
    DEPTH = w_in.shape[0]
    H, DK, DV = state_ret.shape[2:]
    PB, PW = state_pool.shape[2:]
    QL = mla_qnorm_g.shape[1]
    KV = mla_kvnorm_g.shape[1]
    ROW = cache_mla.shape[3]
    ROPE = ROW - KV
    NOPE = mla_qn_g.shape[1] - ROPE // 2
    MH = mla_w_uq.shape[2] // (NOPE + ROPE)
    VD = mla_w_ukv.shape[2] // MH - NOPE
    P0 = page_table.shape[1] * cache_mla.shape[2]
    E = exp_w_gate.shape[1]
    assert E == N_GROUPS * EXPERTS_PER_GROUP and 2 * ROPE == LANES
    Tp, Ts = B * S, Bd * LD
    T = Tp + Ts
    u_off = 2 * H * DK + 2 * H * DV
    cq_off = u_off + PW
    kv_off = cq_off + QL
    kpe_off = kv_off + KV
    NA = kpe_off + ROPE
    NA_pad = -(-(NA + ROPE) // LANES) * LANES
    assert w_in.shape[2] == NA + 3 * D and kpe_off % LANES == 0 and kv_off % KV == 0 and u_off % PW == 0
    dm = dict(B=B, S=S, D=D, Bd=Bd, LD=LD, H=H, DK=DK, DV=DV, PB=PB, PW=PW, QL=QL, KV=KV, ROPE=ROPE,
              NOPE=NOPE, MH=MH, VD=VD, P0=P0, Tp=Tp, T=T, u_off=u_off, cq_off=cq_off, kv_off=kv_off,
              kpe_off=kpe_off)

    pos_p = jnp.arange(S)
    pos_s = P0 + jnp.arange(LD)
    pos_all = jnp.concatenate([jnp.tile(pos_p, B), jnp.tile(pos_s, Bd)])
    nb_ret = _tile(Bd, 8, 1)
    tabs_ret_p = _ret_tables(pos_p, DK)
    tabs_ret_s = _ret_tables(jnp.tile(pos_s, nb_ret), DK)
    tabs_q = _mla_tables(pos_all, ROPE, MH, 0)
    tabs_k = _mla_tables(pos_all, ROPE, 1, ROPE)
    lg = jnp.log1p(-jnp.exp2(-5.0 - jnp.arange(H, dtype=F32)))

    x = jnp.concatenate([x_prompt.reshape(Tp, D), x_sample.reshape(Ts, D)], axis=0)
    tmA = _tile(T, 512)
    tnA = _tile(NA_pad, 1024, LANES)
    tnD = _tile(D, 1024, LANES)
    rows_p, rows_s, ret_p, ret_s, pool_p, pool_s = [], [], [], [], [], []
    cache_t = jnp.swapaxes(cache_mla, 2, 3)
    w_in_t = jnp.swapaxes(w_in, 1, 2)
    state_pool_t = jnp.swapaxes(state_pool, 1, 2)
    w_in_b = w_in_t.astype(BF16)
    w_gates_b = w_in_b[:, NA:]
    w_br_ret_b, w_br_pool_b, w_br_mla_b = (w.astype(BF16) for w in (w_br_ret, w_br_pool, w_br_mla))
    w_out_b = w_out.astype(BF16)
    exp_gate_b, exp_up_b, exp_down_b = (w.astype(BF16) for w in (exp_w_gate, exp_w_up, exp_w_down))
    for l in range(DEPTH):
        lw = {"mla_qnorm_g": mla_qnorm_g[l], "mla_w_uq": mla_w_uq[l], "mla_kvnorm_g": mla_kvnorm_g[l],
              "mla_w_ukv": mla_w_ukv[l], "mla_qn_g": mla_qn_g[l], "mla_kn_g": mla_kn_g[l]}
        h = _rmsnorm(x, norm1_g[l], BF16)
        zA = _matmul(h, w_in_b, l, tm=tmA, tn=tnA, n=NA_pad, b_is_nk=True, name="in_proj_main")
        gl = _matmul(h, w_gates_b, l, tm=tmA, tn=tnD, b_is_nk=True, name="in_proj_gates")
        o_r, s_p, s_s = _retention(zA, lg, ret_gn_g[l], tabs_ret_p, tabs_ret_s, state_ret, l, dm)
        o_p, b_p, b_s = _pooling(zA, pool_w[l], pool_scale[l], state_pool_t, l, dm)
        o_m, rows = _mla(zA, lw, tabs_q, tabs_k, cache_t, l, page_table, dm)
        merged = _merge(o_r, o_p, o_m, w_br_ret_b, w_br_pool_b, w_br_mla_b, l, gl, dm)
        x1 = _matmul(merged, w_out_b, l, tm=tmA, tn=tnD, res=x, name="out_proj")
        wr = jnp.pad(jnp.concatenate([router_g_w[l], router_e_w[l]], axis=1),
                     ((0, 0), (0, ROUTE_LANES - N_GROUPS - E)))
        br = jnp.pad(jnp.concatenate([router_g_b[l], router_e_b[l]]), (0, ROUTE_LANES - N_GROUPS - E))
        x = _moe(x1, norm2_g[l], wr, br.reshape(1, ROUTE_LANES), exp_gate_b, exp_up_b, exp_down_b, l, dm)
        rows_p.append(rows[:Tp].reshape(B, S, ROW))
        rows_s.append(rows[Tp:].reshape(Bd, LD, ROW))
        ret_p.append(s_p)
        ret_s.append(s_s)
        pool_p.append(b_p)
        pool_s.append(b_s)
    return (x[:Tp].reshape(B, S, D), x[Tp:].reshape(Bd, LD, D), jnp.stack(rows_p), jnp.stack(rows_s),
            jnp.stack(ret_p), jnp.stack(ret_s), jnp.stack(pool_p), jnp.swapaxes(jnp.stack(pool_s), 1, 2))
```

```python
import functools

import jax
import jax.numpy as jnp
from jax import lax
from jax.experimental import pallas as pl
from jax.experimental.pallas import tpu as pltpu

F32 = jnp.float32
BF16 = jnp.bfloat16
I32 = jnp.int32

EPS = 1e-6
ROPE_BASE = 10000.0
NEG_INF = -1e30
POOL_WINDOWS = (2, 4, 8, 16)
N_GROUPS = 4
EXPERTS_PER_GROUP = 4
LANES = 128
VMEM_LIMIT = 56 * 1024 * 1024
ROUTE_LANES = 128
MOE_TM = 256
DMA_UNROLL = 8
PAGE_SLOTS = 3


def _cp(*sem):
    return pltpu.CompilerParams(dimension_semantics=sem, vmem_limit_bytes=VMEM_LIMIT)


def _tile(n, pref, mult=8):
    for t in range(min(pref, n), 0, -1):
        if n % t == 0 and t % mult == 0:
            return t
    return n


def _dot(a, b):
    return jnp.dot(a, b, preferred_element_type=F32)


def _dot_nt(a, b):
    return lax.dot_general(a, b, (((1,), (1,)), ((), ())), preferred_element_type=F32)


def _idiv(x, d):
    return jnp.floor((x.astype(F32) + 0.5) * (1.0 / d)).astype(I32)


def _rmsnorm_kernel(x_ref, g_ref, o_ref):
    x = x_ref[...]
    y = x * lax.rsqrt(jnp.mean(x * x, axis=-1, keepdims=True) + EPS)
    o_ref[...] = (y * g_ref[...]).astype(o_ref.dtype)


def _rmsnorm(x, g, out_dtype):
    T, D = x.shape
    tm = _tile(T, 512)
    return pl.pallas_call(
        _rmsnorm_kernel,
        grid=(T // tm,),
        in_specs=[pl.BlockSpec((tm, D), lambda i: (i, 0)), pl.BlockSpec((1, D), lambda i: (0, 0))],
        out_specs=pl.BlockSpec((tm, D), lambda i: (i, 0)),
        out_shape=jax.ShapeDtypeStruct((T, D), out_dtype),
        compiler_params=_cp("parallel"),
        name="rmsnorm",
    )(x, g.reshape(1, D))


def _mm_kernel(a_ref, b_ref, *rest, has_res, b_is_nk):
    if has_res:
        r_ref, o_ref = rest
    else:
        (o_ref,) = rest
    acc = _dot_nt(a_ref[...], b_ref[0]) if b_is_nk else _dot(a_ref[...], b_ref[0])
    if has_res:
        acc = acc + r_ref[...]
    o_ref[...] = acc.astype(o_ref.dtype)


def _matmul(a, b, layer, *, tm, tn, n=None, res=None, out_dtype=F32, b_is_nk=False, name="matmul"):
    M, K = a.shape
    N = n if n is not None else (b.shape[1] if b_is_nk else b.shape[2])
    if b_is_nk:
        b_spec = pl.BlockSpec((1, tn, K), lambda j, i: (layer, j, 0))
    else:
        b_spec = pl.BlockSpec((1, K, tn), lambda j, i: (layer, 0, j))
    in_specs = [pl.BlockSpec((tm, K), lambda j, i: (i, 0)), b_spec]
    args = [a, b]
    if res is not None:
        in_specs.append(pl.BlockSpec((tm, tn), lambda j, i: (i, j)))
        args.append(res)
    return pl.pallas_call(
        functools.partial(_mm_kernel, has_res=res is not None, b_is_nk=b_is_nk),
        grid=(N // tn, M // tm),
        in_specs=in_specs,
        out_specs=pl.BlockSpec((tm, tn), lambda j, i: (i, j)),
        out_shape=jax.ShapeDtypeStruct((M, N), out_dtype),
        compiler_params=_cp("parallel", "parallel"),
        name=name,
    )(*args)


def _ret_core(q, k, v, s_prev, decay, qdec):
    qb = q.astype(BF16)
    inner = _dot_nt(qb, k.astype(BF16)) * decay
    o = _dot(inner.astype(BF16), v.astype(BF16))
    return o, qb


def _ret_epilogue(o, gn, gate):
    mu = jnp.mean(o, axis=-1, keepdims=True)
    d = o - mu
    var = jnp.mean(d * d, axis=-1, keepdims=True)
    y = d * lax.rsqrt(var + EPS) * gn
    return y * (gate * jax.nn.sigmoid(gate))


def _ret_prompt_kernel(lg_ref, q_ref, k_ref, v_ref, gr_ref, cos_ref, sin_ref, gn_ref,
                       o_ref, sfin_ref, st_ref, *, C, DK):
    h = pl.program_id(1)
    c = pl.program_id(2)
    lg = lg_ref[h]

    @pl.when(c == 0)
    def _():
        st_ref[...] = jnp.zeros_like(st_ref)

    cos = cos_ref[...]
    sin = sin_ref[...]
    q = q_ref[...]
    k = k_ref[...]
    q = q * cos + pltpu.roll(q, DK // 2, 1) * sin
    k = (k * cos + pltpu.roll(k, DK // 2, 1) * sin) * (DK ** -0.5)
    v = v_ref[...]
    r = lax.broadcasted_iota(I32, (C, 1), 0)
    col = lax.broadcasted_iota(I32, (1, C), 1)
    diff = r - col
    decay = jnp.where(diff >= 0, jnp.exp(lg * jnp.maximum(diff, 0).astype(F32)), 0.0)
    s_prev = st_ref[...]
    o, qb = _ret_core(q, k, v, s_prev, decay, None)
    o = o + _dot(qb, s_prev.astype(BF16)) * jnp.exp(lg * (r + 1).astype(F32))
    kd = k * jnp.exp(lg * (C - 1 - r).astype(F32))
    s_new = jnp.exp(lg * C) * s_prev + _dot(kd.T.astype(BF16), v.astype(BF16))
    st_ref[...] = s_new
    sfin_ref[0, 0] = s_new
    o_ref[...] = _ret_epilogue(o, gn_ref[...], gr_ref[...]).astype(o_ref.dtype)


def _ret_sample_kernel(lg_ref, q_ref, k_ref, v_ref, gr_ref, cos_ref, sin_ref, gn_ref, s_ref,
                       o_ref, snew_ref, *, NB, LD, DK):
    h = pl.program_id(1)
    lg = lg_ref[h]
    L = NB * LD
    cos = cos_ref[...]
    sin = sin_ref[...]
    q = q_ref[...]
    k = k_ref[...]
    q = q * cos + pltpu.roll(q, DK // 2, 1) * sin
    k = (k * cos + pltpu.roll(k, DK // 2, 1) * sin) * (DK ** -0.5)
    v = v_ref[...]
    r = lax.broadcasted_iota(I32, (L, 1), 0)
    col = lax.broadcasted_iota(I32, (1, L), 1)
    rb = _idiv(r, LD)
    rt = r - rb * LD
    same = rb == _idiv(col, LD)
    diff = r - col
    decay = jnp.where(same & (diff >= 0), jnp.exp(lg * jnp.maximum(diff, 0).astype(F32)), 0.0)
    o, qb = _ret_core(q, k, v, None, decay, None)
    qdec = jnp.exp(lg * (rt + 1).astype(F32))
    kd = k * jnp.exp(lg * (LD - 1 - rt).astype(F32))
    vb = v.astype(BF16)
    sdec = jnp.exp(lg * LD)
    for s in range(NB):
        mine = rb == s
        s_prev = s_ref[0, s, 0]
        o = o + jnp.where(mine, _dot(qb, s_prev.astype(BF16)) * qdec, 0.0)
        kds = jnp.where(mine, kd, 0.0)
        snew_ref[s, 0] = sdec * s_prev + _dot(kds.T.astype(BF16), vb)
    o_ref[...] = _ret_epilogue(o, gn_ref[...], gr_ref[...]).astype(o_ref.dtype)


def _retention(zA, lg, gn, tabs_p, tabs_s, state, layer, dm):
    H, DK, DV, B, S, Bd, LD, Tp = dm["H"], dm["DK"], dm["DV"], dm["B"], dm["S"], dm["Bd"], dm["LD"], dm["Tp"]
    C = _tile(S, 256)
    nc = S // C
    gn2 = gn.reshape(1, H * DV)
    o_p, s_p = pl.pallas_call(
        functools.partial(_ret_prompt_kernel, C=C, DK=DK),
        grid_spec=pltpu.PrefetchScalarGridSpec(
            num_scalar_prefetch=1,
            grid=(B, H, nc),
            in_specs=[
                pl.BlockSpec((C, DK), lambda b, h, c, lg: (b * nc + c, h)),
                pl.BlockSpec((C, DK), lambda b, h, c, lg: (b * nc + c, H + h)),
                pl.BlockSpec((C, DV), lambda b, h, c, lg: (b * nc + c, 2 * H + h)),
                pl.BlockSpec((C, DV), lambda b, h, c, lg: (b * nc + c, 3 * H + h)),
                pl.BlockSpec((C, DK), lambda b, h, c, lg: (c, 0)),
                pl.BlockSpec((C, DK), lambda b, h, c, lg: (c, 0)),
                pl.BlockSpec((1, DV), lambda b, h, c, lg: (0, h)),
            ],
            out_specs=[
                pl.BlockSpec((C, DV), lambda b, h, c, lg: (b * nc + c, h)),
                pl.BlockSpec((1, 1, DK, DV), lambda b, h, c, lg: (b, h, 0, 0)),
            ],
            scratch_shapes=[pltpu.VMEM((DK, DV), F32)],
        ),
        out_shape=[jax.ShapeDtypeStruct((Tp, H * DV), BF16),
                   jax.ShapeDtypeStruct((B, H, DK, DV), F32)],
        compiler_params=_cp("parallel", "parallel", "arbitrary"),
        name="retention_prompt",
    )(lg, zA, zA, zA, zA, tabs_p[0], tabs_p[1], gn2)

    NB = _tile(Bd, 8, 1)
    L = NB * LD
    off = Tp // L
    o_s, s_s = pl.pallas_call(
        functools.partial(_ret_sample_kernel, NB=NB, LD=LD, DK=DK),
        grid_spec=pltpu.PrefetchScalarGridSpec(
            num_scalar_prefetch=1,
            grid=(Bd // NB, H),
            in_specs=[
                pl.BlockSpec((L, DK), lambda i, h, lg: (off + i, h)),
                pl.BlockSpec((L, DK), lambda i, h, lg: (off + i, H + h)),
                pl.BlockSpec((L, DV), lambda i, h, lg: (off + i, 2 * H + h)),
                pl.BlockSpec((L, DV), lambda i, h, lg: (off + i, 3 * H + h)),
                pl.BlockSpec((L, DK), lambda i, h, lg: (0, 0)),
                pl.BlockSpec((L, DK), lambda i, h, lg: (0, 0)),
                pl.BlockSpec((1, DV), lambda i, h, lg: (0, h)),
                pl.BlockSpec((1, NB, 1, DK, DV), lambda i, h, lg: (layer, i, h, 0, 0)),
            ],
            out_specs=[
                pl.BlockSpec((L, DV), lambda i, h, lg: (i, h)),
                pl.BlockSpec((NB, 1, DK, DV), lambda i, h, lg: (i, h, 0, 0)),
            ],
        ),
        out_shape=[jax.ShapeDtypeStruct((Bd * LD, H * DV), BF16),
                   jax.ShapeDtypeStruct((Bd, H, DK, DV), F32)],
        compiler_params=_cp("parallel", "parallel"),
        name="retention_sample",
    )(lg, zA, zA, zA, zA, tabs_s[0], tabs_s[1], gn2, state)
    return jnp.concatenate([o_p, o_s], axis=0), s_p, s_s


def _pool_prompt_kernel(u_ref, halo_ref, w_ref, sc_ref, o_ref, full_ref, *, TL, HALO, GC):
    i = pl.program_id(1)
    u = u_ref[...]
    full_ref[pl.ds(HALO, TL), :] = u
    full_ref[pl.ds(0, HALO), :] = jnp.where(i == 0, 0.0, halo_ref[...])
    pos = i * TL + lax.broadcasted_iota(I32, (TL, 1), 0)
    for g, w in enumerate(POOL_WINDOWS):
        cs = slice(g * GC, (g + 1) * GC)
        win = full_ref[pl.ds(HALO, TL), cs]
        for j in range(1, w):
            win = win + full_ref[pl.ds(HALO - j, TL), cs]
        cnt = jnp.minimum(w, pos + 1).astype(F32)
        pooled = win / cnt - u[:, cs]
        z = _dot(pooled.astype(BF16), w_ref[g])
        o_ref[:, cs] = (z * sc_ref[:, cs]).astype(o_ref.dtype)


def _pool_sample_kernel(u_ref, buf_ref, w_ref, sc_ref, o_ref, nbuf_ref, *, LD, PB, GC, pos0):
    rows = [buf_ref[0, j] for j in range(PB)] + [u_ref[:, t, :] for t in range(LD)]
    for j in range(PB):
        nbuf_ref[j] = rows[LD + j]
    for g, w in enumerate(POOL_WINDOWS):
        cs = slice(g * GC, (g + 1) * GC)
        pooled = []
        for t in range(LD):
            win = rows[PB + t][:, cs]
            for j in range(1, w):
                win = win + rows[PB + t - j][:, cs]
            cnt = float(min(w, pos0 + t + 1))
            pooled.append(win / cnt - rows[PB + t][:, cs])
        z = _dot(jnp.concatenate(pooled, axis=0).astype(BF16), w_ref[g])
        nb = u_ref.shape[0]
        for t in range(LD):
            o_ref[:, t, cs] = (z[t * nb:(t + 1) * nb] * sc_ref[:, cs]).astype(o_ref.dtype)


def _pooling(zA, pool_w, pool_scale, buf, layer, dm):
    B, S, Bd, LD, Tp, PW, PB = dm["B"], dm["S"], dm["Bd"], dm["LD"], dm["Tp"], dm["PW"], dm["PB"]
    G = len(POOL_WINDOWS)
    GC = PW // G
    HALO = 16
    assert PB < HALO and S % HALO == 0 and max(POOL_WINDOWS) <= HALO
    TL = _tile(S, 512, HALO)
    nt = S // TL
    ucol = dm["u_off"] // PW
    wb = pool_w.astype(BF16)
    sc = pool_scale.reshape(1, PW)
    o_p = pl.pallas_call(
        functools.partial(_pool_prompt_kernel, TL=TL, HALO=HALO, GC=GC),
        grid=(B, nt),
        in_specs=[
            pl.BlockSpec((TL, PW), lambda b, i: (b * nt + i, ucol)),
            pl.BlockSpec((HALO, PW), lambda b, i: (jnp.maximum((b * nt + i) * (TL // HALO) - 1, 0), ucol)),
            pl.BlockSpec((G, GC, GC), lambda b, i: (0, 0, 0)),
            pl.BlockSpec((1, PW), lambda b, i: (0, 0)),
        ],
        out_specs=pl.BlockSpec((TL, PW), lambda b, i: (b * nt + i, 0)),
        out_shape=jax.ShapeDtypeStruct((Tp, PW), BF16),
        scratch_shapes=[pltpu.VMEM((TL + HALO, PW), F32)],
        compiler_params=_cp("parallel", "parallel"),
        name="pool_prompt",
    )(zA, zA, wb, sc)

    u_s = zA[Tp:, dm["u_off"]:dm["u_off"] + PW].reshape(Bd, LD, PW)
    NB = _tile(Bd, 32, 8)
    o_s, nbuf = pl.pallas_call(
        functools.partial(_pool_sample_kernel, LD=LD, PB=PB, GC=GC, pos0=dm["P0"]),
        grid=(Bd // NB,),
        in_specs=[
            pl.BlockSpec((NB, LD, PW), lambda i: (i, 0, 0)),
            pl.BlockSpec((1, PB, NB, PW), lambda i: (layer, 0, i, 0)),
            pl.BlockSpec((G, GC, GC), lambda i: (0, 0, 0)),
            pl.BlockSpec((1, PW), lambda i: (0, 0)),
        ],
        out_specs=[pl.BlockSpec((NB, LD, PW), lambda i: (i, 0, 0)),
                   pl.BlockSpec((PB, NB, PW), lambda i: (0, i, 0))],
        out_shape=[jax.ShapeDtypeStruct((Bd, LD, PW), BF16), jax.ShapeDtypeStruct((PB, Bd, PW), F32)],
        compiler_params=_cp("parallel"),
        name="pool_sample",
    )(u_s, buf, wb, sc)
    pool_p = zA[:Tp, dm["u_off"]:dm["u_off"] + PW].reshape(B, S, PW)[:, S - PB:]
    return jnp.concatenate([o_p, o_s.reshape(Bd * LD, PW)], axis=0), pool_p, nbuf


def _qprep_kernel(cq_ref, g_ref, w_ref, cos_ref, sa_ref, sb_ref, gn_ref, gr_ref, q_ref,
                  *, MH, NOPE, ROPE, scale):
    x = cq_ref[...]
    y = x * lax.rsqrt(jnp.mean(x * x, axis=-1, keepdims=True) + EPS) * g_ref[...]
    qq = _dot(y.astype(BF16), w_ref[...])
    qn = qq[:, :MH * NOPE]
    qr = qq[:, MH * NOPE:]
    W = MH * ROPE
    qr = (qr * cos_ref[...] + pltpu.roll(qr, W - ROPE // 2, 1) * sa_ref[...]
          + pltpu.roll(qr, ROPE // 2, 1) * sb_ref[...])
    gn = gn_ref[...]
    gr = gr_ref[...]
    for h in range(MH):
        a = qn[:, h * NOPE:(h + 1) * NOPE]
        b = qr[:, h * ROPE:(h + 1) * ROPE]
        ss = jnp.sum(a * a, axis=-1, keepdims=True) + jnp.sum(b * b, axis=-1, keepdims=True)
        inv = lax.rsqrt(ss * (1.0 / (NOPE + ROPE)) + EPS) * scale
        q_ref[h, :, :NOPE] = (a * inv * gn).astype(q_ref.dtype)
        q_ref[h, :, NOPE:] = (b * inv * gr).astype(q_ref.dtype)


def _kvprep_kernel(ckv_ref, kpe_ref, g_ref, wuk_ref, wuv_ref, cos_ref, sa_ref, sb_ref,
                   rows_ref, k_ref, v_ref, *, MH, NOPE, ROPE, KV):
    x = ckv_ref[...]
    lat = x * lax.rsqrt(jnp.mean(x * x, axis=-1, keepdims=True) + EPS) * g_ref[...]
    kp = kpe_ref[...]
    W = kp.shape[1]
    kp = (kp * cos_ref[...] + pltpu.roll(kp, W - ROPE // 2, 1) * sa_ref[...]
          + pltpu.roll(kp, ROPE // 2, 1) * sb_ref[...])
    rows_ref[:, :KV] = lat
    rows_ref[:, KV:] = kp[:, :ROPE]
    latb = lat.astype(BF16)
    kn = _dot(latb, wuk_ref[...])
    v_ref[...] = _dot(latb, wuv_ref[...]).astype(v_ref.dtype)
    sspe = jnp.sum(kp * kp, axis=-1, keepdims=True)
    for h in range(MH):
        a = kn[:, h * NOPE:(h + 1) * NOPE]
        ss = jnp.sum(a * a, axis=-1, keepdims=True) + sspe
        inv = lax.rsqrt(ss * (1.0 / (NOPE + ROPE)) + EPS)
        k_ref[h, :, :NOPE] = (a * inv).astype(k_ref.dtype)
        k_ref[h, :, NOPE:] = (kp[:, :ROPE] * inv).astype(k_ref.dtype)


def _attn_prompt_kernel(q_ref, k_ref, v_ref, o_ref, *, TQ):
    i = pl.program_id(2)
    q = q_ref[0]
    qpos = i * TQ + lax.broadcasted_iota(I32, (TQ, 1), 0)

    def body(j, carry):
        m, l, acc = carry
        start = pl.multiple_of(j * TQ, TQ)
        s = _dot_nt(q, k_ref[0, pl.ds(start, TQ), :])
        kpos = start + lax.broadcasted_iota(I32, (1, TQ), 1)
        s = jnp.where(kpos <= qpos, s, NEG_INF)
        m_new = jnp.maximum(m, jnp.max(s, axis=-1, keepdims=True))
        corr = jnp.exp(m - m_new)
        p = jnp.exp(s - m_new)
        l = l * corr + jnp.sum(p, axis=-1, keepdims=True)
        acc = acc * corr + _dot(p.astype(BF16), v_ref[pl.ds(start, TQ), :])
        return m_new, l, acc

    init = (jnp.full((TQ, 1), NEG_INF, F32), jnp.zeros((TQ, 1), F32), jnp.zeros(o_ref.shape, F32))
    _, l, acc = lax.fori_loop(0, i + 1, body, init)
    o_ref[...] = (acc / l).astype(o_ref.dtype)


def _attn_sample_kernel(pt_ref, qn_ref, qr_ref, self_ref, cache_hbm, wukt_ref, wuv_ref, o_ref,
                        qc_ref, m_ref, l_ref, acc_ref, pbuf, sem,
                        *, PP, MH, NOPE, ROPE, KV, LD, NP_STEPS, NB, layer):
    b = pl.program_id(0)
    step = pl.program_id(1)
    NQ = LD * MH
    n = b * NP_STEPS + step
    slot = lax.rem(n, PAGE_SLOTS)

    def page_copy(bb, ss, sl, i):
        return pltpu.make_async_copy(cache_hbm.at[layer, pt_ref[bb, ss * PP + i]], pbuf.at[sl, i], sem.at[sl])

    def fetch(ahead):
        wrap = step + ahead >= NP_STEPS
        bb = jnp.where(wrap, b + 1, b)
        ss = jnp.where(wrap, step + ahead - NP_STEPS, step + ahead)
        sl = lax.rem(n + ahead, PAGE_SLOTS)
        for i in range(PP):
            page_copy(bb, ss, sl, i).start()

    @pl.when(n == 0)
    def _():
        for ahead in range(PAGE_SLOTS - 1):
            fetch(ahead)

    @pl.when(n + PAGE_SLOTS - 1 < NB * NP_STEPS)
    def _():
        fetch(PAGE_SLOTS - 1)

    for i in range(PP):
        page_copy(b, step, slot, i).wait()

    def absorb(chunks, mask):
        s_parts, lat_parts = [], []
        for page_t in chunks:
            W = page_t.shape[1]
            pb = page_t.astype(BF16)
            latb = pb[:KV]
            both = _dot(qc_ref[...], latb)
            kn = both[:MH * NOPE]
            kpe = page_t[KV:]
            ss = jnp.sum((kn * kn).reshape(MH, NOPE, W), axis=1) + jnp.sum(kpe * kpe, axis=0, keepdims=True)
            rinv = lax.rsqrt(ss * (1.0 / (NOPE + ROPE)) + EPS)
            s = both[MH * NOPE:] + _dot(qr_ref[0], pb[KV:])
            s_parts.append(s * jnp.concatenate([rinv] * LD, axis=0))
            lat_parts.append(latb)
        s = s_parts[0] if len(s_parts) == 1 else jnp.concatenate(s_parts, axis=1)
        if mask is not None:
            s = jnp.where(mask, s, NEG_INF)
        m_old = m_ref[...]
        m_new = jnp.maximum(m_old, jnp.max(s, axis=-1, keepdims=True))
        corr = jnp.exp(m_old - m_new)
        p = jnp.exp(s - m_new)
        l_ref[...] = l_ref[...] * corr + jnp.sum(p, axis=-1, keepdims=True)
        pb16 = p.astype(BF16)
        pv, off = None, 0
        for latb in lat_parts:
            W = latb.shape[1]
            part = _dot_nt(pb16[:, off:off + W], latb)
            pv = part if pv is None else pv + part
            off += W
        acc_ref[...] = acc_ref[...] * corr + pv
        m_ref[...] = m_new

    @pl.when(step == 0)
    def _():
        m_ref[...] = jnp.full_like(m_ref, NEG_INF)
        l_ref[...] = jnp.zeros_like(l_ref)
        acc_ref[...] = jnp.zeros_like(acc_ref)
        qc_ref[pl.ds(0, MH * NOPE), :] = wukt_ref[...]
        qc_ref[pl.ds(MH * NOPE, NQ), :] = _dot(qn_ref[0], wukt_ref[...]).astype(BF16)
        page_t = self_ref[0]
        t = _idiv(lax.broadcasted_iota(I32, (NQ, 1), 0), MH)
        j = lax.broadcasted_iota(I32, (1, page_t.shape[1]), 1)
        absorb([page_t], j <= t)

    if PP % 2 == 0:
        absorb([jnp.concatenate([pbuf[slot, c], pbuf[slot, c + 1]], axis=1) for c in range(0, PP, 2)], None)
    else:
        absorb([pbuf[slot, c] for c in range(PP)], None)

    @pl.when(step == NP_STEPS - 1)
    def _():
        o_lat = (acc_ref[...] / l_ref[...]).astype(BF16)
        VD = wuv_ref.shape[1] // MH
        hrow = lax.broadcasted_iota(I32, (MH, MH * VD), 0)
        hcol = _idiv(lax.broadcasted_iota(I32, (MH, MH * VD), 1), VD)
        for t in range(LD):
            full = _dot(o_lat[t * MH:(t + 1) * MH], wuv_ref[...])
            o_ref[0, pl.ds(t, 1), :] = jnp.sum(jnp.where(hrow == hcol, full, 0.0), axis=0,
                                              keepdims=True).astype(o_ref.dtype)


def _mla(zA, lw, tabs_q, tabs_k, cache_t, layer, page_table, dm):
    T, Tp, B, S, Bd, LD = dm["T"], dm["Tp"], dm["B"], dm["S"], dm["Bd"], dm["LD"]
    MH, NOPE, ROPE, KV, QL, VD = dm["MH"], dm["NOPE"], dm["ROPE"], dm["KV"], dm["QL"], dm["VD"]
    QK = NOPE + ROPE
    ROW = KV + ROPE
    scale = QK ** -0.5
    tm = _tile(T, 512)

    w_uq = lw["mla_w_uq"].reshape(QL, MH, QK)
    w_uq = jnp.concatenate([w_uq[:, :, :NOPE].reshape(QL, MH * NOPE),
                            w_uq[:, :, NOPE:].reshape(QL, MH * ROPE)], axis=1).astype(BF16)
    w_ukv = lw["mla_w_ukv"].reshape(KV, MH, NOPE + VD)
    w_uk = w_ukv[:, :, :NOPE].reshape(KV, MH * NOPE).astype(BF16)
    w_uv = w_ukv[:, :, NOPE:].reshape(KV, MH * VD).astype(BF16)
    gq, gk = lw["mla_qn_g"], lw["mla_kn_g"]
    gn = (gq[:NOPE] * gk[:NOPE]).reshape(1, NOPE)
    gr = jnp.tile(gq[NOPE:] * gk[NOPE:], 2).reshape(1, ROPE)

    c_q = zA[:, dm["cq_off"]:dm["cq_off"] + QL]
    q = pl.pallas_call(
        functools.partial(_qprep_kernel, MH=MH, NOPE=NOPE, ROPE=ROPE, scale=scale),
        grid=(T // tm,),
        in_specs=[
            pl.BlockSpec((tm, QL), lambda i: (i, 0)),
            pl.BlockSpec((1, QL), lambda i: (0, 0)),
            pl.BlockSpec((QL, MH * QK), lambda i: (0, 0)),
            pl.BlockSpec((tm, MH * ROPE), lambda i: (i, 0)),
            pl.BlockSpec((tm, MH * ROPE), lambda i: (i, 0)),
            pl.BlockSpec((tm, MH * ROPE), lambda i: (i, 0)),
            pl.BlockSpec((1, NOPE), lambda i: (0, 0)),
            pl.BlockSpec((1, ROPE), lambda i: (0, 0)),
        ],
        out_specs=pl.BlockSpec((MH, tm, QK), lambda i: (0, i, 0)),
        out_shape=jax.ShapeDtypeStruct((MH, T, QK), BF16),
        compiler_params=_cp("parallel"),
        name="mla_q_prep",
    )(c_q, lw["mla_qnorm_g"].reshape(1, QL), w_uq, tabs_q[0], tabs_q[1], tabs_q[2], gn, gr)

    kvb = dm["kv_off"] // KV
    kpb = dm["kpe_off"] // (2 * ROPE)
    rows, k, v = pl.pallas_call(
        functools.partial(_kvprep_kernel, MH=MH, NOPE=NOPE, ROPE=ROPE, KV=KV),
        grid=(T // tm,),
        in_specs=[
            pl.BlockSpec((tm, KV), lambda i: (i, kvb)),
            pl.BlockSpec((tm, 2 * ROPE), lambda i: (i, kpb)),
            pl.BlockSpec((1, KV), lambda i: (0, 0)),
            pl.BlockSpec((KV, MH * NOPE), lambda i: (0, 0)),
            pl.BlockSpec((KV, MH * VD), lambda i: (0, 0)),
            pl.BlockSpec((tm, 2 * ROPE), lambda i: (i, 0)),
            pl.BlockSpec((tm, 2 * ROPE), lambda i: (i, 0)),
            pl.BlockSpec((tm, 2 * ROPE), lambda i: (i, 0)),
        ],
        out_specs=[
            pl.BlockSpec((tm, ROW), lambda i: (i, 0)),
            pl.BlockSpec((MH, tm, QK), lambda i: (0, i, 0)),
            pl.BlockSpec((tm, MH * VD), lambda i: (i, 0)),
        ],
        out_shape=[jax.ShapeDtypeStruct((T, ROW), F32),
                   jax.ShapeDtypeStruct((MH, T, QK), BF16),
                   jax.ShapeDtypeStruct((T, MH * VD), BF16)],
        compiler_params=_cp("parallel"),
        name="mla_kv_prep",
    )(zA, zA, lw["mla_kvnorm_g"].reshape(1, KV), w_uk, w_uv, tabs_k[0], tabs_k[1], tabs_k[2])

    TQ = _tile(S, 512)
    nq = S // TQ
    o_p = pl.pallas_call(
        functools.partial(_attn_prompt_kernel, TQ=TQ),
        grid=(B, MH, nq),
        in_specs=[
            pl.BlockSpec((1, TQ, QK), lambda b, h, i: (h, b * nq + i, 0)),
            pl.BlockSpec((1, S, QK), lambda b, h, i: (h, b, 0)),
            pl.BlockSpec((S, VD), lambda b, h, i: (b, h)),
        ],
        out_specs=pl.BlockSpec((TQ, VD), lambda b, h, i: (b * nq + i, h)),
        out_shape=jax.ShapeDtypeStruct((Tp, MH * VD), BF16),
        compiler_params=_cp("parallel", "parallel", "parallel"),
        name="mla_attn_prompt",
    )(q, k, v)

    PAGE = cache_t.shape[3]
    NP = page_table.shape[1]
    PP = _tile(NP, 16, 1)
    NQ = LD * MH
    qs = q[:, Tp:].reshape(MH, Bd, LD, QK).transpose(1, 2, 0, 3).reshape(Bd, NQ, QK)
    head_of_row = jnp.arange(NQ) % MH
    eye = (head_of_row[:, None] == jnp.arange(MH)[None, :])
    qn_bd = jnp.where(eye[None, :, :, None], qs[:, :, None, :NOPE], 0).reshape(Bd, NQ, MH * NOPE)
    qr_s = qs[:, :, NOPE:]
    self_t = jnp.pad(rows[Tp:].reshape(Bd, LD, ROW).transpose(0, 2, 1), ((0, 0), (0, 0), (0, PAGE - LD)))
    assert NP // PP >= PAGE_SLOTS - 1
    o_s = pl.pallas_call(
        functools.partial(_attn_sample_kernel, PP=PP, MH=MH, NOPE=NOPE, ROPE=ROPE, KV=KV, LD=LD,
                          NP_STEPS=NP // PP, NB=Bd, layer=layer),
        grid_spec=pltpu.PrefetchScalarGridSpec(
            num_scalar_prefetch=1,
            grid=(Bd, NP // PP),
            in_specs=[
                pl.BlockSpec((1, NQ, MH * NOPE), lambda b, s, pt: (b, 0, 0)),
                pl.BlockSpec((1, NQ, ROPE), lambda b, s, pt: (b, 0, 0)),
                pl.BlockSpec((1, ROW, PAGE), lambda b, s, pt: (b, 0, 0)),
                pl.BlockSpec(memory_space=pl.ANY),
                pl.BlockSpec((MH * NOPE, KV), lambda b, s, pt: (0, 0)),
                pl.BlockSpec((KV, MH * VD), lambda b, s, pt: (0, 0)),
            ],
            out_specs=pl.BlockSpec((1, LD, MH * VD), lambda b, s, pt: (b, 0, 0)),
            scratch_shapes=[pltpu.VMEM((MH * NOPE + NQ, KV), BF16), pltpu.VMEM((NQ, 1), F32), pltpu.VMEM((NQ, 1), F32),
                            pltpu.VMEM((NQ, KV), F32), pltpu.VMEM((PAGE_SLOTS, PP, ROW, PAGE), F32),
                            pltpu.SemaphoreType.DMA((PAGE_SLOTS,))],
        ),
        out_shape=jax.ShapeDtypeStruct((Bd, LD, MH * VD), BF16),
        compiler_params=_cp("arbitrary", "arbitrary"),
        name="mla_attn_sample",
    )(page_table, qn_bd, qr_s, self_t, cache_t, w_uk.T, w_uv)
    o_m = jnp.concatenate([o_p, o_s.reshape(Bd * LD, MH * VD)], axis=0)
    return o_m, rows


def _merge_kernel(or_ref, op_ref, om_ref, wr_ref, wp_ref, wm_ref, g0_ref, g1_ref, g2_ref, o_ref):
    acc = jax.nn.sigmoid(g0_ref[...]) * _dot(or_ref[...], wr_ref[0])
    acc = acc + jax.nn.sigmoid(g1_ref[...]) * _dot(op_ref[...], wp_ref[0])
    acc = acc + jax.nn.sigmoid(g2_ref[...]) * _dot(om_ref[...], wm_ref[0])
    o_ref[...] = acc.astype(o_ref.dtype)


def _merge(o_r, o_p, o_m, w_r, w_p, w_m, layer, gl, dm):
    T, D = dm["T"], dm["D"]
    tm = _tile(T, 512)
    tn = _tile(D, 1024, LANES)
    nj = D // tn
    branch = lambda w: pl.BlockSpec((tm, w.shape[1]), lambda j, i: (i, 0))
    weight = lambda w: pl.BlockSpec((1, w.shape[1], tn), lambda j, i: (layer, 0, j))
    gate = lambda n: pl.BlockSpec((tm, tn), lambda j, i: (i, n * nj + j))
    return pl.pallas_call(
        _merge_kernel,
        grid=(nj, T // tm),
        in_specs=[branch(w_r), branch(w_p), branch(w_m), weight(w_r), weight(w_p), weight(w_m),
                  gate(0), gate(1), gate(2)],
        out_specs=pl.BlockSpec((tm, tn), lambda j, i: (i, j)),
        out_shape=jax.ShapeDtypeStruct((T, D), BF16),
        compiler_params=_cp("parallel", "parallel"),
        name="gated_merge",
    )(o_r, o_p, o_m, w_r, w_p, w_m, gl, gl, gl)


def _router_kernel(x_ref, g_ref, w_ref, b_ref, h_ref, r_ref):
    x = x_ref[...]
    h = x * lax.rsqrt(jnp.mean(x * x, axis=-1, keepdims=True) + EPS) * g_ref[...]
    half = x.shape[1] // 2
    bits = lax.bitcast_convert_type(h.astype(BF16).astype(F32), jnp.uint32)
    h_ref[...] = (bits[:, :half] >> 16) | (bits[:, half:] & jnp.uint32(0xFFFF0000))
    logits = jnp.dot(h, w_ref[...], preferred_element_type=F32, precision=lax.Precision.HIGHEST) + b_ref[...]
    lane = lax.broadcasted_iota(I32, logits.shape, 1)
    big = jnp.int32(1 << 20)
    gl = jnp.where(lane < N_GROUPS, logits, -jnp.inf)
    gmax = jnp.max(gl, axis=-1, keepdims=True)
    g_idx = jnp.min(jnp.where(gl == gmax, lane, big), axis=-1, keepdims=True)
    g_w = 1.0 / jnp.sum(jnp.exp(gl - gmax), axis=-1, keepdims=True)
    lo = N_GROUPS + g_idx * EXPERTS_PER_GROUP
    sel = (lane >= lo) & (lane < lo + EXPERTS_PER_GROUP)
    el = jnp.where(sel, logits, -jnp.inf)
    pe = jnp.exp(el - jnp.max(el, axis=-1, keepdims=True))
    p = pe / jnp.sum(pe, axis=-1, keepdims=True)
    p = jnp.where(sel, p, -1.0)
    p1 = jnp.max(p, axis=-1, keepdims=True)
    i1 = jnp.min(jnp.where(p == p1, lane, big), axis=-1, keepdims=True)
    prest = jnp.where(lane == i1, -1.0, p)
    p2 = jnp.max(prest, axis=-1, keepdims=True)
    i2 = jnp.min(jnp.where(prest == p2, lane, big), axis=-1, keepdims=True)
    den = p1 + p2
    w1 = g_w * p1 / den
    w2 = g_w * p2 / den
    out = jnp.where(lane == 0, (i1 - N_GROUPS).astype(F32),
                    jnp.where(lane == 1, (i2 - N_GROUPS).astype(F32),
                              jnp.where(lane == 2, w1, jnp.where(lane == 3, w2, 0.0))))
    r_ref[...] = out


def _moe_up_kernel(te_ref, tok_ref, nu_ref, h_hbm, wg_ref, wu_ref, a_ref, xbuf, sem, *, TM):
    i = pl.program_id(0)
    n_used = nu_ref[0]
    slot = i % 2

    def row_copy(s, r, tok):
        return pltpu.make_async_copy(h_hbm.at[pl.ds(tok, 1)], xbuf.at[s, pl.ds(r, 1)], sem.at[s])

    def gather(tile, s):
        def issue(r, c):
            row_copy(s, r, tok_ref[tile * TM + r]).start()
            return c

        lax.fori_loop(0, TM, issue, 0, unroll=DMA_UNROLL)

    @pl.when((i == 0) & (n_used > 0))
    def _():
        gather(0, 0)

    @pl.when(i + 1 < n_used)
    def _():
        gather(i + 1, 1 - slot)

    @pl.when(i < n_used)
    def _():
        def drain(r, c):
            row_copy(slot, r, 0).wait()
            return c

        lax.fori_loop(0, TM, drain, 0, unroll=DMA_UNROLL)
        w = xbuf[slot]
        x = jnp.concatenate([lax.bitcast_convert_type(w << 16, F32),
                             lax.bitcast_convert_type(w & jnp.uint32(0xFFFF0000), F32)], axis=1).astype(BF16)
        g = _dot(x, wg_ref[0, 0])
        u = _dot(x, wu_ref[0, 0])
        a_ref[...] = (g * jax.nn.sigmoid(g) * u).astype(a_ref.dtype)

    @pl.when(i >= n_used)
    def _():
        a_ref[...] = jnp.zeros_like(a_ref)


def _moe_down_kernel(te_ref, nu_ref, a_ref, wd_ref, rw_ref, y_ref):
    i = pl.program_id(0)

    @pl.when(i < nu_ref[0])
    def _():
        y_ref[...] = _dot(a_ref[...], wd_ref[0, 0]) * rw_ref[...]

    @pl.when(i >= nu_ref[0])
    def _():
        y_ref[...] = jnp.zeros_like(y_ref)


def _moe_combine_kernel(p1_ref, p2_ref, x_ref, y_hbm, o_ref, b1, b2, sem, *, TM, NT):
    i = pl.program_id(0)
    slot = i % 2

    def row_copy(s, src_row, dst, r):
        return pltpu.make_async_copy(y_hbm.at[pl.ds(src_row, 1)], dst.at[s, pl.ds(r, 1)], sem.at[s])

    def gather(tile, s):
        def issue(r, c):
            row_copy(s, p1_ref[tile * TM + r], b1, r).start()
            row_copy(s, p2_ref[tile * TM + r], b2, r).start()
            return c

        lax.fori_loop(0, TM, issue, 0, unroll=DMA_UNROLL)

    @pl.when(i == 0)
    def _():
        gather(0, 0)

    @pl.when(i + 1 < NT)
    def _():
        gather(i + 1, 1 - slot)

    def drain(r, c):
        row_copy(slot, 0, b1, r).wait()
        row_copy(slot, 0, b2, r).wait()
        return c

    lax.fori_loop(0, TM, drain, 0, unroll=DMA_UNROLL)
    o_ref[...] = x_ref[...] + (b1[slot] + b2[slot])


def _moe(x1, norm_g, wr, br, wg, wu, wd, layer, dm):
    T, D = dm["T"], dm["D"]
    E, DE = wg.shape[1], wg.shape[3]
    tm = _tile(T, 256)
    h2, route = pl.pallas_call(
        _router_kernel,
        grid=(T // tm,),
        in_specs=[
            pl.BlockSpec((tm, D), lambda i: (i, 0)),
            pl.BlockSpec((1, D), lambda i: (0, 0)),
            pl.BlockSpec((D, ROUTE_LANES), lambda i: (0, 0)),
            pl.BlockSpec((1, ROUTE_LANES), lambda i: (0, 0)),
        ],
        out_specs=[pl.BlockSpec((tm, D // 2), lambda i: (i, 0)), pl.BlockSpec((tm, ROUTE_LANES), lambda i: (i, 0))],
        out_shape=[jax.ShapeDtypeStruct((T, D // 2), jnp.uint32), jax.ShapeDtypeStruct((T, ROUTE_LANES), F32)],
        compiler_params=_cp("parallel"),
        name="moe_router",
    )(x1, norm_g.reshape(1, D), wr, br)

    TM = MOE_TM
    n_tiles = pl.cdiv(2 * T, TM) + E
    P = n_tiles * TM
    ids = route[:, :2].astype(I32).reshape(-1)
    wts = route[:, 2:4].reshape(-1)
    onehot = (ids[:, None] == jnp.arange(E, dtype=I32)[None, :]).astype(I32)
    csum = jnp.cumsum(onehot, axis=0)
    rank = jnp.sum(jnp.where(onehot > 0, csum, 0), axis=1) - 1
    counts = csum[-1]
    padded = ((counts + TM - 1) // TM) * TM
    ends = jnp.cumsum(padded)
    starts = ends - padded
    pos = (starts[ids] + rank).astype(I32)
    tok_and_w = jnp.stack([(jnp.arange(2 * T, dtype=I32) // 2).astype(F32), wts], axis=1)
    row_meta = jnp.zeros((P, 2), F32).at[pos].set(tok_and_w)
    row_tok = row_meta[:, 0].astype(I32)
    row_w = row_meta[:, 1:2]
    tile_start = jnp.arange(n_tiles, dtype=I32) * TM
    tile_e = jnp.minimum(jnp.sum((ends[None, :] <= tile_start[:, None]).astype(I32), axis=1), E - 1)
    n_used = (ends[-1] // TM).astype(I32).reshape(1)
    pos2 = pos.reshape(T, 2)

    act = pl.pallas_call(
        functools.partial(_moe_up_kernel, TM=TM),
        grid_spec=pltpu.PrefetchScalarGridSpec(
            num_scalar_prefetch=3,
            grid=(n_tiles,),
            in_specs=[
                pl.BlockSpec(memory_space=pl.ANY),
                pl.BlockSpec((1, 1, D, DE), lambda i, te, tok, nu: (layer, te[i], 0, 0)),
                pl.BlockSpec((1, 1, D, DE), lambda i, te, tok, nu: (layer, te[i], 0, 0)),
            ],
            out_specs=pl.BlockSpec((TM, DE), lambda i, te, tok, nu: (i, 0)),
            scratch_shapes=[pltpu.VMEM((2, TM, D // 2), jnp.uint32), pltpu.SemaphoreType.DMA((2,))],
        ),
        out_shape=jax.ShapeDtypeStruct((P, DE), BF16),
        compiler_params=_cp("arbitrary"),
        name="moe_gather_up",
    )(tile_e, row_tok, n_used, h2, wg, wu)

    ys = pl.pallas_call(
        _moe_down_kernel,
        grid_spec=pltpu.PrefetchScalarGridSpec(
            num_scalar_prefetch=2,
            grid=(n_tiles,),
            in_specs=[
                pl.BlockSpec((TM, DE), lambda i, te, nu: (i, 0)),
                pl.BlockSpec((1, 1, DE, D), lambda i, te, nu: (layer, te[i], 0, 0)),
                pl.BlockSpec((TM, 1), lambda i, te, nu: (i, 0)),
            ],
            out_specs=pl.BlockSpec((TM, D), lambda i, te, nu: (i, 0)),
        ),
        out_shape=jax.ShapeDtypeStruct((P, D), F32),
        compiler_params=_cp("arbitrary"),
        name="moe_down",
    )(tile_e, n_used, act, wd, row_w)

    tc = _tile(T, 256)
    return pl.pallas_call(
        functools.partial(_moe_combine_kernel, TM=tc, NT=T // tc),
        grid_spec=pltpu.PrefetchScalarGridSpec(
            num_scalar_prefetch=2,
            grid=(T // tc,),
            in_specs=[
                pl.BlockSpec((tc, D), lambda i, p1, p2: (i, 0)),
                pl.BlockSpec(memory_space=pl.ANY),
            ],
            out_specs=pl.BlockSpec((tc, D), lambda i, p1, p2: (i, 0)),
            scratch_shapes=[pltpu.VMEM((2, tc, D), F32), pltpu.VMEM((2, tc, D), F32),
                            pltpu.SemaphoreType.DMA((2,))],
        ),
        out_shape=jax.ShapeDtypeStruct((T, D), F32),
        compiler_params=_cp("arbitrary"),
        name="moe_combine",
    )(pos2[:, 0], pos2[:, 1], x1, ys)


def _rope_cos_sin(pos, half):
    inv = ROPE_BASE ** (-jnp.arange(half, dtype=F32) / half)
    ang = pos.astype(F32)[:, None] * inv[None, :]
    return jnp.cos(ang), jnp.sin(ang)


def _ret_tables(pos, dk):
    cos, sin = _rope_cos_sin(pos, dk // 2)
    return jnp.concatenate([cos, cos], axis=1), jnp.concatenate([-sin, sin], axis=1)


def _mla_tables(pos, rope, reps, pad):
    cos, sin = _rope_cos_sin(pos, rope // 2)
    z = jnp.zeros_like(sin)
    tabs = (jnp.concatenate([cos, cos], axis=1), jnp.concatenate([-sin, z], axis=1),
            jnp.concatenate([z, sin], axis=1))
    if pad:
        return tuple(jnp.pad(t, ((0, 0), (0, pad))) for t in tabs)
    return tuple(jnp.tile(t, (1, reps)) for t in tabs)


def kernel(x_prompt, x_sample, cache_mla, page_table, state_ret, state_pool, norm1_g, w_in, ret_gn_g, pool_w, pool_scale, mla_qnorm_g, mla_w_uq, mla_kvnorm_g, mla_w_ukv, mla_qn_g, mla_kn_g, w_br_ret, w_br_pool, w_br_mla, w_out, norm2_g, router_g_w, router_g_b, router_e_w, router_e_b, exp_w_gate, exp_w_up, exp_w_down):
    B, S, D = x_prompt.shape
    Bd, LD, _ = x_sample.shape
    DEPTH = w_in.shape[0]
    H, DK, DV = state_ret.shape[2:]
    PB, PW = state_pool.shape[2:]
    QL = mla_qnorm_g.shape[1]
    KV = mla_kvnorm_g.shape[1]
    ROW = cache_mla.shape[3]
    ROPE = ROW - KV
    NOPE = mla_qn_g.shape[1] - ROPE // 2
    MH = mla_w_uq.shape[2] // (NOPE + ROPE)
    VD = mla_w_ukv.shape[2] // MH - NOPE
    P0 = page_table.shape[1] * cache_mla.shape[2]
    E = exp_w_gate.shape[1]
    assert E == N_GROUPS * EXPERTS_PER_GROUP and 2 * ROPE == LANES
    Tp, Ts = B * S, Bd * LD
    T = Tp + Ts
    u_off = 2 * H * DK + 2 * H * DV
    cq_off = u_off + PW
    kv_off = cq_off + QL
    kpe_off = kv_off + KV
    NA = kpe_off + ROPE
    NA_pad = -(-(NA + ROPE) // LANES) * LANES
    assert w_in.shape[2] == NA + 3 * D and kpe_off % LANES == 0 and kv_off % KV == 0 and u_off % PW == 0
    dm = dict(B=B, S=S, D=D, Bd=Bd, LD=LD, H=H, DK=DK, DV=DV, PB=PB, PW=PW, QL=QL, KV=KV, ROPE=ROPE,
              NOPE=NOPE, MH=MH, VD=VD, P0=P0, Tp=Tp, T=T, u_off=u_off, cq_off=cq_off, kv_off=kv_off,
              kpe_off=kpe_off)

    pos_p = jnp.arange(S)
    pos_s = P0 + jnp.arange(LD)
    pos_all = jnp.concatenate([jnp.tile(pos_p, B), jnp.tile(pos_s, Bd)])
    nb_ret = _tile(Bd, 8, 1)
    tabs_ret_p = _ret_tables(pos_p, DK)
    tabs_ret_s = _ret_tables(jnp.tile(pos_s, nb_ret), DK)
    tabs_q = _mla_tables(pos_all, ROPE, MH, 0)
    tabs_k = _mla_tables(pos_all, ROPE, 1, ROPE)
    lg = jnp.log1p(-jnp.exp2(-5.0 - jnp.arange(H, dtype=F32)))

    x = jnp.concatenate([x_prompt.reshape(Tp, D), x_sample.reshape(Ts, D)], axis=0)
    tmA = _tile(T, 512)
    tnA = _tile(NA_pad, 1024, LANES)
    tnD = _tile(D, 1024, LANES)
    rows_p, rows_s, ret_p, ret_s, pool_p, pool_s = [], [], [], [], [], []
    cache_t = jnp.swapaxes(cache_mla, 2, 3)
    w_in_t = jnp.swapaxes(w_in, 1, 2)
    state_pool_t = jnp.swapaxes(state_pool, 1, 2)
    w_in_b = w_in_t.astype(BF16)
    w_gates_b = w_in_b[:, NA:]
    w_br_ret_b, w_br_pool_b, w_br_mla_b = (w.astype(BF16) for w in (w_br_ret, w_br_pool, w_br_mla))
    w_out_b = w_out.astype(BF16)
    exp_gate_b, exp_up_b, exp_down_b = (w.astype(BF16) for w in (exp_w_gate, exp_w_up, exp_w_down))
    for l in range(DEPTH):
        lw = {"mla_qnorm_g": mla_qnorm_g[l], "mla_w_uq": mla_w_uq[l], "mla_kvnorm_g": mla_kvnorm_g[l],
              "mla_w_ukv": mla_w_ukv[l], "mla_qn_g": mla_qn_g[l], "mla_kn_g": mla_kn_g[l]}
        h = _rmsnorm(x, norm1_g[l], BF16)
        zA = _matmul(h, w_in_b, l, tm=tmA, tn=tnA, n=NA_pad, b_is_nk=True, name="in_proj_main")
        gl = _matmul(h, w_gates_b, l, tm=tmA, tn=tnD, b_is_nk=True, name="in_proj_gates")
        o_r, s_p, s_s = _retention(zA, lg, ret_gn_g[l], tabs_ret_p, tabs_ret_s, state_ret, l, dm)
        o_p, b_p, b_s = _pooling(zA, pool_w[l], pool_scale[l], state_pool_t, l, dm)
        o_m, rows = _mla(zA, lw, tabs_q, tabs_k, cache_t, l, page_table, dm)
        merged = _merge(o_r, o_p, o_m, w_br_ret_b, w_br_pool_b, w_br_mla_b, l, gl, dm)
        x1 = _matmul(merged, w_out_b, l, tm=tmA, tn=tnD, res=x, name="out_proj")
        wr = jnp.pad(jnp.concatenate([router_g_w[l], router_e_w[l]], axis=1),
                     ((0, 0), (0, ROUTE_LANES - N_GROUPS - E)))
        br = jnp.pad(jnp.concatenate([router_g_b[l], router_e_b[l]]), (0, ROUTE_LANES - N_GROUPS - E))
        x = _moe(x1, norm2_g[l], wr, br.reshape(1, ROUTE_LANES), exp_gate_b, exp_up_b, exp_down_b, l, dm)
        rows_p.append(rows[:Tp].reshape(B, S, ROW))
        rows_s.append(rows[Tp:].reshape(Bd, LD, ROW))
        ret_p.append(s_p)
        ret_s.append(s_s)
        pool_p.append(b_p)
        pool_s.append(b_s)
    return (x[:Tp].reshape(B, S, D), x[Tp:].reshape(Bd, LD, D), jnp.stack(rows_p), jnp.stack(rows_s),
            jnp.stack(ret_p), jnp.stack(ret_s), jnp.stack(pool_p), jnp.swapaxes(jnp.stack(pool_s), 1, 2))
```
